```python
import math
import jax, jax.numpy as jnp
from jax import lax
import numpy as np

D_MODEL = 1024
BATCH = 8
SEQ = 2048
DEPTH = 2
DEC_BATCH = 128
DEC_SEQ = 4
PAST_LEN = 16384
PAGE_SIZE = 128

CHUNK = 64
BR_W = D_MODEL // 2
N_BRANCH = 3
M_HEADS = 4
M_HD = BR_W // M_HEADS
M_W = M_HEADS * M_HD
G_HEADS = 4
G_DK = BR_W // (2 * G_HEADS)
G_DV = BR_W // G_HEADS
G_KW = G_HEADS * G_DK
G_VW = G_HEADS * G_DV
G_RANK = 16
G_TAU = 16.0
S_HD = 64
S_HEADS = BR_W // S_HD
S_W = S_HEADS * S_HD
S_GROUPS = 2
S_HG = S_HEADS // S_GROUPS
S_STATE = 64
S_CONV = 4
S_XBC = S_W + 2 * S_GROUPS * S_STATE
P_HEADS = 8
P_NKEYS = 128
P_EXPERTS = P_NKEYS * P_NKEYS
P_QDIM = 256
P_HALF = P_QDIM // 2
P_TOPK = 16
P_BLOCK = 128
DN_ALPHA = (2.0 * DEPTH) ** 0.25
DN_BETA = (8.0 * DEPTH) ** -0.25
EPS = 1e-5

IN_SIZES = (M_W, M_W, M_W, M_W, M_HEADS, M_HEADS,
            G_KW, G_KW, G_VW, G_VW, G_RANK,
            S_W, S_XBC, S_HEADS,
            N_BRANCH * D_MODEL)
IN_W = sum(IN_SIZES)

kernel_name = 'hybrid_mlstm_gla_ssd_peer_step'


def _split_cols(z):
    out, start = [], 0
    for size in IN_SIZES:
        out.append(z[..., start:start + size])
        start += size
    return out


def _chunk_len(L):
    c = min(CHUNK, L)
    while L % c:
        c -= 1
    return c


def _to_chunks(a, c):
    B, L = a.shape[:2]
    a = a.reshape((B, L // c, c) + a.shape[2:])
    return jnp.moveaxis(a, 1, 0)


def _from_chunks(a):
    a = jnp.moveaxis(a, 0, 1)
    return a.reshape((a.shape[0], a.shape[1] * a.shape[2]) + a.shape[3:])


def _rms(x):
    return x * lax.rsqrt(jnp.mean(jnp.square(x), -1, keepdims=True) + EPS)


def _layernorm(x, g, b):
    xf = x.astype(jnp.float32)
    mu = xf.mean(-1, keepdims=True)
    var = jnp.mean(jnp.square(xf - mu), -1, keepdims=True)
    return (xf - mu) * lax.rsqrt(var + EPS) * g.astype(jnp.float32) + b.astype(jnp.float32)


def _mlstm(q, k, v, log_i, log_f, C0, n0, m0):
    c = _chunk_len(q.shape[1])
    causal = jnp.tril(jnp.ones((c, c), bool))

    def step(carry, xs):
        C, n, m = carry
        qc, kc, vc, lic, lfc = xs
        b = jnp.cumsum(lfc, axis=1)
        a = b + m[:, None, :]
        d = b[:, :, None, :] - b[:, None, :, :] + lic[:, None, :, :]
        d = jnp.where(causal[None, :, :, None], d, -jnp.inf)
        m_t = jnp.maximum(a, jnp.max(d, axis=2))
        s = jnp.einsum('bthd,bshd->btsh', qc, kc) * jnp.exp(d - m_t[:, :, None, :])
        e_in = jnp.exp(a - m_t)
        num = (jnp.einsum('btsh,bshd->bthd', s, vc)
               + e_in[..., None] * jnp.einsum('bhvd,bthd->bthv', C, qc))
        den = jnp.sum(s, axis=2) + e_in * jnp.einsum('bhd,bthd->bth', n, qc)
        h = num / jnp.maximum(jnp.abs(den), jnp.exp(-m_t))[..., None]
        b_last = b[:, -1, :]
        g = b_last[:, None, :] - b + lic
        m_new = jnp.maximum(b_last + m, jnp.max(g, axis=1))
        e_c = jnp.exp(b_last + m - m_new)
        wg = jnp.exp(g - m_new[:, None, :])
        C_new = e_c[..., None, None] * C + jnp.einsum('bsh,bshv,bshd->bhvd', wg, vc, kc)
        n_new = e_c[..., None] * n + jnp.einsum('bsh,bshd->bhd', wg, kc)
        return (C_new, n_new, m_new), h

    xs = tuple(_to_chunks(t, c) for t in (q, k, v, log_i, log_f))
    (C, n, m), h = lax.scan(step, (C0, n0, m0), xs)
    return _from_chunks(h), C, n, m


def _gla(q, k, v, log_a, S0):
    c = _chunk_len(q.shape[1])
    causal = jnp.tril(jnp.ones((c, c), bool))

    def step(S, xs):
        qc, kc, vc, lac = xs
        lam = jnp.cumsum(lac, axis=1)
        diff = lam[:, :, None] - lam[:, None, :]
        decay = jnp.exp(jnp.where(causal[None, :, :, None, None], diff, -jnp.inf))
        att = jnp.einsum('bthk,bshk,btshk->btsh', qc, kc, decay)
        o = (jnp.einsum('btsh,bshv->bthv', att, vc)
             + jnp.einsum('bthk,bhkv->bthv', qc * jnp.exp(lam), S))
        lam_last = lam[:, -1]
        S_new = (jnp.exp(lam_last)[..., None] * S
                 + jnp.einsum('bshk,bshv->bhkv', kc * jnp.exp(lam_last[:, None] - lam), vc))
        return S_new, o

    xs = tuple(_to_chunks(t, c) for t in (q, k, v, log_a))
    S, o = lax.scan(step, S0, xs)
    return _from_chunks(o), S


def _ssd(x, dt, Bm, Cm, A, h0):
    c = _chunk_len(x.shape[1])
    causal = jnp.tril(jnp.ones((c, c), bool))

    def step(h, xs):
        xc, dtc, bc, cc = xs
        lam = jnp.cumsum(dtc * A, axis=1)
        diff = lam[:, :, None] - lam[:, None, :]
        decay = jnp.exp(jnp.where(causal[None, :, :, None, None], diff, -jnp.inf))
        cb = jnp.einsum('btgn,bsgn->btsg', cc, bc)
        w = cb[..., None] * decay * dtc[:, None]
        y = (jnp.einsum('btsgh,bsghp->btghp', w, xc)
             + jnp.einsum('btgn,bghpn->btghp', cc, h) * jnp.exp(lam)[..., None])
        lam_last = lam[:, -1]
        ws = jnp.exp(lam_last[:, None] - lam) * dtc
        h_new = (jnp.exp(lam_last)[..., None, None] * h
                 + jnp.einsum('bsgh,bsghp,bsgn->bghpn', ws, xc, bc))
        return h_new, y

    xs = tuple(_to_chunks(t, c) for t in (x, dt, Bm, Cm))
    h, y = lax.scan(step, h0, xs)
    return _from_chunks(y), h


def _peer(x, p_wq, p_keys, p_u, p_v):
    f32 = jnp.float32
    B, L, D = x.shape
    T = B * L
    xf = x.reshape(T, D)
    q = jnp.matmul(xf, p_wq).astype(f32).reshape(T, P_HEADS, 2, P_HALF)
    sc = jnp.einsum('thjc,hjkc->thjk', q, p_keys.astype(f32))
    s, i = lax.top_k(sc, P_TOPK)
    cand = s[:, :, 0, :, None] + s[:, :, 1, None, :]
    cidx = i[:, :, 0, :, None] * P_NKEYS + i[:, :, 1, None, :]
    top_s, pos = lax.top_k(cand.reshape(T, P_HEADS, P_TOPK * P_TOPK), P_TOPK)
    eidx = jnp.take_along_axis(cidx.reshape(T, P_HEADS, P_TOPK * P_TOPK), pos, axis=-1)
    g = jax.nn.softmax(top_s, axis=-1)
    nb = -(-T // P_BLOCK)
    pad = nb * P_BLOCK - T
    xb = jnp.pad(xf, ((0, pad), (0, 0))).reshape(nb, P_BLOCK, D)
    eb = jnp.pad(eidx.reshape(T, P_HEADS * P_TOPK), ((0, pad), (0, 0))).reshape(nb, P_BLOCK, -1)
    gb = jnp.pad(g.reshape(T, P_HEADS * P_TOPK), ((0, pad), (0, 0))).reshape(nb, P_BLOCK, -1)

    def block(args):
        xt, et, gt = args
        u = jnp.take(p_u, et, axis=0)
        act = jax.nn.gelu(jnp.einsum('tkd,td->tk', u, xt).astype(f32), approximate=False) * gt
        vv = jnp.take(p_v, et, axis=0)
        return jnp.einsum('tk,tkd->td', act, vv.astype(f32))

    out = lax.map(block, (xb, eb, gb))
    return out.reshape(nb * P_BLOCK, D)[:T].reshape(B, L, D)


def _layer(x, C0, n0, m0, S0, h0, buf0,
           w_in, m_i_bias, m_f_bias, m_norm, g_a_up, g_a_bias, g_norm,
           s_conv_w, s_conv_b, s_dt_bias, s_A_log, s_D, s_norm,
           w_branch, w_out, ln1_g, ln1_b, p_wq, p_keys, p_u, p_v, ln2_g, ln2_b):
    f32 = jnp.float32
    B, L, _ = x.shape
    z = jnp.matmul(x, w_in).astype(f32)
    (mq, mk, mv, mo, mi, mf, gq, gk, gv, gr, ga, sz, sxbc, sdt, gate) = _split_cols(z)

    q = mq.reshape(B, L, M_HEADS, M_HD) * (M_HD ** -0.5)
    k = mk.reshape(B, L, M_HEADS, M_HD)
    v = mv.reshape(B, L, M_HEADS, M_HD)
    log_i = mi + m_i_bias.astype(f32)
    log_f = jax.nn.log_sigmoid(mf + m_f_bias.astype(f32))
    hm, C1, n1, m1 = _mlstm(q, k, v, log_i, log_f,
                            C0.astype(f32), n0.astype(f32), m0.astype(f32))
    mu = hm.mean(-1, keepdims=True)
    hm = (hm - mu) * lax.rsqrt(jnp.mean(jnp.square(hm - mu), -1, keepdims=True) + EPS)
    hm = hm.reshape(B, L, M_W) * m_norm.astype(f32) * jax.nn.sigmoid(mo)

    q = gq.reshape(B, L, G_HEADS, G_DK) * (G_DK ** -0.5)
    k = gk.reshape(B, L, G_HEADS, G_DK)
    v = gv.reshape(B, L, G_HEADS, G_DV)
    log_a = jax.nn.log_sigmoid(jnp.matmul(ga, g_a_up.astype(f32)) + g_a_bias.astype(f32)) / G_TAU
    hg, S1 = _gla(q, k, v, log_a.reshape(B, L, G_HEADS, G_DK), S0.astype(f32))
    hg = _rms(hg).reshape(B, L, G_VW) * g_norm.astype(f32) * jax.nn.silu(gr)

    xp = jnp.concatenate([buf0.astype(f32), sxbc], axis=1)
    cw = s_conv_w.astype(f32)
    conv = s_conv_b.astype(f32) + sum(cw[j] * xp[:, j:j + L] for j in range(S_CONV))
    buf1 = xp[:, L:]
    act = jax.nn.silu(conv)
    xs = act[..., :S_W].reshape(B, L, S_GROUPS, S_HG, S_HD)
    Bm = act[..., S_W:S_W + S_GROUPS * S_STATE].reshape(B, L, S_GROUPS, S_STATE)
    Cm = act[..., S_W + S_GROUPS * S_STATE:].reshape(B, L, S_GROUPS, S_STATE)
    dt = jax.nn.softplus(sdt + s_dt_bias.astype(f32)).reshape(B, L, S_GROUPS, S_HG)
    A = -jnp.exp(s_A_log.astype(f32)).reshape(S_GROUPS, S_HG)
    ys, h1 = _ssd(xs, dt, Bm, Cm, A,
                  h0.astype(f32).reshape(B, S_GROUPS, S_HG, S_HD, S_STATE))
    ys = ys + s_D.astype(f32).reshape(S_GROUPS, S_HG, 1) * xs
    ys = ys.reshape(B, L, S_W) * jax.nn.silu(sz)
    hs = _rms(ys.reshape(B, L, S_GROUPS, S_W // S_GROUPS)).reshape(B, L, S_W) * s_norm.astype(f32)
    h1 = h1.reshape(B, S_HEADS, S_HD, S_STATE)

    br = jnp.stack([hm, hg, hs], axis=2)
    proj = jnp.einsum('blnc,ncd->blnd', br, w_branch.astype(f32))
    gates = jax.nn.sigmoid(gate.reshape(B, L, N_BRANCH, D_MODEL))
    mixed = jnp.sum(gates * proj, axis=2)
    y = jnp.matmul(mixed, w_out.astype(f32))
    h = _layernorm(DN_ALPHA * x.astype(f32) + y, ln1_g, ln1_b)

    h = _layernorm(DN_ALPHA * h + _peer(h, p_wq, p_keys, p_u, p_v), ln2_g, ln2_b)
    return (h.astype(x.dtype), C1.astype(C0.dtype), n1.astype(n0.dtype), m1.astype(m0.dtype),
            S1.astype(S0.dtype), h1.astype(h0.dtype), buf1.astype(buf0.dtype))


def _zero_states(b, dtype):
    return (jnp.zeros((b, M_HEADS, M_HD, M_HD), dtype),
            jnp.zeros((b, M_HEADS, M_HD), dtype),
            jnp.zeros((b, M_HEADS), dtype),
            jnp.zeros((b, G_HEADS, G_DK, G_DV), dtype),
            jnp.zeros((b, S_HEADS, S_HD, S_STATE), dtype),
            jnp.zeros((b, S_CONV - 1, S_XBC), dtype))


def setup_inputs(seed: int = 0) -> dict:
    key = jax.random.key(seed)
    ks = iter(jax.random.split(key, 48))
    nrm = lambda shape, s=1.0: jax.random.normal(next(ks), shape, jnp.float32) * s
    uni = lambda shape, lo, hi: jax.random.uniform(next(ks), shape, jnp.float32, lo, hi)
    Dd = DEPTH
    dt0 = jnp.exp(uni((Dd, S_HEADS), math.log(1e-3), math.log(1e-1)))
    return {
        'x_prompt': nrm((BATCH, SEQ, D_MODEL)),
        'x_sample': nrm((DEC_BATCH, DEC_SEQ, D_MODEL)),
        'state_mlstm_C': nrm((Dd, DEC_BATCH, M_HEADS, M_HD, M_HD), 0.1),
        'state_mlstm_n': nrm((Dd, DEC_BATCH, M_HEADS, M_HD), 0.5),
        'state_mlstm_m': uni((Dd, DEC_BATCH, M_HEADS), 0.0, 3.0),
        'state_gla_S': nrm((Dd, DEC_BATCH, G_HEADS, G_DK, G_DV), 0.1),
        'state_ssm_h': nrm((Dd, DEC_BATCH, S_HEADS, S_HD, S_STATE), 0.1),
        'state_conv': nrm((Dd, DEC_BATCH, S_CONV - 1, S_XBC)),
        'w_in': nrm((Dd, D_MODEL, IN_W), D_MODEL ** -0.5),
        'm_i_bias': nrm((Dd, M_HEADS), 0.1),
        'm_f_bias': uni((Dd, M_HEADS), 3.0, 6.0),
        'm_norm': 1.0 + nrm((Dd, M_W), 0.02),
        'g_a_up': nrm((Dd, G_RANK, G_KW), G_RANK ** -0.5),
        'g_a_bias': nrm((Dd, G_KW), 0.1),
        'g_norm': 1.0 + nrm((Dd, G_VW), 0.02),
        's_conv_w': nrm((Dd, S_CONV, S_XBC), S_CONV ** -0.5),
        's_conv_b': nrm((Dd, S_XBC), 0.01),
        's_dt_bias': dt0 + jnp.log(-jnp.expm1(-dt0)),
        's_A_log': jnp.log(uni((Dd, S_HEADS), 1.0, 16.0)),
        's_D': 1.0 + nrm((Dd, S_HEADS), 0.1),
        's_norm': 1.0 + nrm((Dd, S_W), 0.02),
        'w_branch': nrm((Dd, N_BRANCH, BR_W, D_MODEL), DN_BETA * BR_W ** -0.5),
        'w_out': nrm((Dd, D_MODEL, D_MODEL), DN_BETA * D_MODEL ** -0.5),
        'ln1_g': 1.0 + nrm((Dd, D_MODEL), 0.01),
        'ln1_b': nrm((Dd, D_MODEL), 0.01),
        'p_wq': nrm((Dd, D_MODEL, P_HEADS * P_QDIM), D_MODEL ** -0.5),
        'p_keys': nrm((Dd, P_HEADS, 2, P_NKEYS, P_HALF), P_HALF ** -0.5),
        'p_u': nrm((Dd, P_EXPERTS, D_MODEL), D_MODEL ** -0.5),
        'p_v': nrm((Dd, P_EXPERTS, D_MODEL), DN_BETA * (P_HEADS * P_TOPK) ** -0.5),
        'ln2_g': 1.0 + nrm((Dd, D_MODEL), 0.01),
        'ln2_b': nrm((Dd, D_MODEL), 0.01),
    }


def reference(x_prompt, x_sample, state_mlstm_C, state_mlstm_n, state_mlstm_m,
              state_gla_S, state_ssm_h, state_conv,
              w_in, m_i_bias, m_f_bias, m_norm, g_a_up, g_a_bias, g_norm,
              s_conv_w, s_conv_b, s_dt_bias, s_A_log, s_D, s_norm,
              w_branch, w_out, ln1_g, ln1_b, p_wq, p_keys, p_u, p_v, ln2_g, ln2_b):
    hp, hs = x_prompt, x_sample
    new_p = [[] for _ in range(6)]
    new_s = [[] for _ in range(6)]
    for l in range(DEPTH):
        w = (w_in[l], m_i_bias[l], m_f_bias[l], m_norm[l], g_a_up[l], g_a_bias[l], g_norm[l],
             s_conv_w[l], s_conv_b[l], s_dt_bias[l], s_A_log[l], s_D[l], s_norm[l],
             w_branch[l], w_out[l], ln1_g[l], ln1_b[l], p_wq[l], p_keys[l], p_u[l], p_v[l],
             ln2_g[l], ln2_b[l])
        outp = _layer(hp, *_zero_states(hp.shape[0], state_mlstm_C.dtype), *w)
        outs = _layer(hs, state_mlstm_C[l], state_mlstm_n[l], state_mlstm_m[l],
                      state_gla_S[l], state_ssm_h[l], state_conv[l], *w)
        hp, hs = outp[0], outs[0]
        for j in range(6):
            new_p[j].append(outp[j + 1])
            new_s[j].append(outs[j + 1])
    P = [jnp.stack(a, axis=0) for a in new_p]
    S = [jnp.stack(a, axis=0) for a in new_s]
    return (hp, hs, P[0], P[1], P[2], P[3], P[4], P[5], S[0], S[1], S[2], S[3], S[4], S[5])
```

```python
import functools
import math

import jax
import jax.numpy as jnp
from jax import lax
from jax.experimental import pallas as pl
from jax.experimental.pallas import tpu as pltpu

f32, bf16, i32 = jnp.float32, jnp.bfloat16, jnp.int32
NEG_INF = float("-inf")

D_MODEL = 1024
DEPTH = 2
BR_W = 512
N_BRANCH = 3
M_HEADS, M_HD = 4, 128
G_HEADS, G_DK, G_DV = 4, 64, 128
G_KW, G_VW = G_HEADS * G_DK, G_HEADS * G_DV
G_RANK = 16
G_TAU = 16.0
S_HD, S_HEADS, S_GROUPS, S_STATE, S_CONV = 64, 8, 2, 64, 4
S_HG = S_HEADS // S_GROUPS
S_W = S_HEADS * S_HD
S_XBC = S_W + 2 * S_GROUPS * S_STATE
P_HEADS, P_NKEYS, P_HALF, P_TOPK = 8, 128, 128, 16
P_EXPERTS = P_NKEYS * P_NKEYS
DN_ALPHA = (2.0 * DEPTH) ** 0.25
EPS = 1e-5

LANE = 128

Z_MQ, Z_MK, Z_MV, Z_MO = 0, 512, 1024, 1536
Z_GQ, Z_GK, Z_GV, Z_GR = 2048, 2304, 2560, 3072
Z_SZ, Z_XBC, Z_SMALL = 3584, 4096, 4864
Z_W = Z_SMALL + LANE
SM_MI, SM_MF, SM_GA, SM_DT = 0, 4, 8, 24

VMEM_LIMIT = 56 * 1024 * 1024

_NT = (((1,), (1,)), ((), ()))
_TN = (((0,), (0,)), ((), ()))
_HI = lax.Precision.HIGHEST


def _cparams(*sem):
    return pltpu.CompilerParams(dimension_semantics=sem, vmem_limit_bytes=VMEM_LIMIT)


def _mm_body(x_ref, w_ref, o_ref):
    o_ref[...] = jnp.dot(x_ref[...].astype(bf16), w_ref[...], preferred_element_type=f32)


def _matmul(x, w, tm):
    T, K = x.shape
    N = w.shape[1]
    return pl.pallas_call(
        _mm_body,
        grid=(T // tm,),
        in_specs=[pl.BlockSpec((tm, K), lambda i: (i, 0)), pl.BlockSpec((K, N), lambda i: (0, 0))],
        out_specs=pl.BlockSpec((tm, N), lambda i: (i, 0)),
        out_shape=jax.ShapeDtypeStruct((T, N), f32),
        compiler_params=_cparams("parallel"),
        name="in_proj",
    )(x, w)


def _wq_body(x_ref, w_ref, o_ref):
    r = jnp.dot(x_ref[...].astype(bf16), w_ref[...], preferred_element_type=f32)
    for g in range(2 * P_HEADS):
        o_ref[g] = r[:, g * P_HALF:(g + 1) * P_HALF]


def _wq_matmul(x, w, tm):
    T, K = x.shape
    G = 2 * P_HEADS
    return pl.pallas_call(
        _wq_body,
        grid=(T // tm,),
        in_specs=[pl.BlockSpec((tm, K), lambda i: (i, 0)), pl.BlockSpec((K, G * P_HALF), lambda i: (0, 0))],
        out_specs=pl.BlockSpec((G, tm, P_HALF), lambda i: (0, i, 0)),
        out_shape=jax.ShapeDtypeStruct((G, T, P_HALF), f32),
        compiler_params=_cparams("parallel"),
        name="peer_query",
    )(x, w)


def _mixer_body(c, valid,
                z_ref, C0, n0, m0, S0, h0, buf0,
                bias_s, alog, mnorm, gnorm, snorm, drow, gup, gab, cw, cb,
                br_ref, C1, n1, m1, S1, h1, buf1,
                C_sc, n_sc, m_sc, S_sc, h_sc, xp_sc):
    j = pl.program_id(1)

    @pl.when(j == 0)
    def _load_state():
        C_sc[...] = C0[0]
        n_sc[...] = n0[0]
        for h in range(M_HEADS):
            m_sc[h:h + 1, :] = jnp.broadcast_to(m0[0, :, h:h + 1], (1, LANE))
        S_sc[...] = S0[0]
        h_sc[...] = h0[0]
        xp_sc[5:8, :] = buf0[0]

    row = lax.broadcasted_iota(i32, (c, c), 0)
    col = lax.broadcasted_iota(i32, (c, c), 1)
    causal = row >= col
    tri = causal.astype(f32)
    lane = lax.broadcasted_iota(i32, (1, LANE), 1)
    tcol = lax.broadcasted_iota(i32, (c, 1), 0)
    tvalid = tcol < valid

    zs = z_ref[0, :, Z_SMALL:Z_SMALL + LANE]
    small = zs + bias_s[...]
    is_lf = (lane >> 2) == (SM_MF >> 2)
    is_dt = (lane >> 3) == (SM_DT >> 3)
    LI = jnp.where(tvalid, small, NEG_INF)
    LF = jnp.where(tvalid, jnp.where(is_lf, jax.nn.log_sigmoid(small), 0.0), 0.0)
    DT = jnp.where(tvalid, jnp.where(is_dt, jax.nn.softplus(small), 0.0), 0.0)
    a_row = jnp.where(is_dt, -jnp.exp(alog[...]), 0.0)
    cum = jnp.dot(tri, LF + DT * a_row, precision=_HI, preferred_element_type=f32)
    cumT = cum.T
    LIT = LI.T
    DTT = DT.T

    for h in range(M_HEADS):
        sl = slice(h * M_HD, (h + 1) * M_HD)
        q = z_ref[0, :, Z_MQ + h * M_HD:Z_MQ + (h + 1) * M_HD] * (M_HD ** -0.5)
        k = z_ref[0, :, Z_MK + h * M_HD:Z_MK + (h + 1) * M_HD]
        v = z_ref[0, :, Z_MV + h * M_HD:Z_MV + (h + 1) * M_HD]
        qb, kb, vb = q.astype(bf16), k.astype(bf16), v.astype(bf16)
        b_c = cum[:, SM_MF + h:SM_MF + h + 1]
        b_r = cumT[SM_MF + h:SM_MF + h + 1, :]
        li_c = LI[:, SM_MI + h:SM_MI + h + 1]
        li_r = LIT[SM_MI + h:SM_MI + h + 1, :]
        m_prev = m_sc[h:h + 1, 0:1]
        a = b_c + m_prev
        d = jnp.where(causal, b_c - b_r + li_r, NEG_INF)
        m_t = jnp.maximum(a, jnp.max(d, axis=1, keepdims=True))
        s = lax.dot_general(qb, kb, _NT, preferred_element_type=f32) * jnp.exp(d - m_t)
        e_in = jnp.exp(a - m_t)
        Cm_ = C_sc[h]
        num = (jnp.dot(s.astype(bf16), vb, preferred_element_type=f32)
               + e_in * lax.dot_general(qb, Cm_.astype(bf16), _NT, preferred_element_type=f32))
        n_row = n_sc[h:h + 1, :]
        den = jnp.sum(s, axis=1, keepdims=True) + e_in * jnp.sum(q * n_row, axis=1, keepdims=True)
        hh = num / jnp.maximum(jnp.abs(den), jnp.exp(-m_t))
        b_last = b_c[c - 1:c, :]
        g_c = b_last - b_c + li_c
        m_new = jnp.maximum(b_last + m_prev, jnp.max(g_c, axis=0, keepdims=True))
        e_c = jnp.exp(b_last + m_prev - m_new)
        wg = jnp.exp(g_c - m_new)
        C_sc[h] = e_c * Cm_ + lax.dot_general((wg * v).astype(bf16), kb, _TN, preferred_element_type=f32)
        n_sc[h:h + 1, :] = e_c * n_row + jnp.sum(wg * k, axis=0, keepdims=True)
        m_sc[h:h + 1, :] = jnp.broadcast_to(m_new, (1, LANE))
        mu = jnp.mean(hh, axis=1, keepdims=True)
        xc = hh - mu
        hn = xc * lax.rsqrt(jnp.mean(xc * xc, axis=1, keepdims=True) + EPS)
        mo = z_ref[0, :, Z_MO + h * M_HD:Z_MO + (h + 1) * M_HD]
        br_ref[0, :, sl] = hn * mnorm[:, sl] * jax.nn.sigmoid(mo)

    ga = jnp.dot(zs.astype(bf16), gup[...], preferred_element_type=f32) + gab[...]
    log_a = jnp.where(tvalid, jax.nn.log_sigmoid(ga) * (1.0 / G_TAU), 0.0)
    lam = jnp.dot(tri, log_a, precision=_HI, preferred_element_type=f32)
    lamT = lam.T
    gq = z_ref[0, :, Z_GQ:Z_GQ + G_KW] * (G_DK ** -0.5)
    gk = z_ref[0, :, Z_GK:Z_GK + G_KW]
    att = [None] * G_HEADS
    m_half = c // 2
    while m_half >= 1:
        sh = int(math.log2(2 * m_half))
        upper = (tcol & (2 * m_half - 1)) >= m_half
        sel = (col == ((row >> sh) << sh) + (m_half - 1)).astype(f32)
        beta = jnp.dot(sel, lam, precision=_HI, preferred_element_type=f32)
        qm = (gq * jnp.exp(jnp.where(upper, lam - beta, NEG_INF))).astype(bf16)
        km = (gk * jnp.exp(jnp.where(upper, NEG_INF, beta - lam))).astype(bf16)
        same_blk = (row >> sh) == (col >> sh)
        for h in range(G_HEADS):
            ks = slice(h * G_DK, (h + 1) * G_DK)
            p = lax.dot_general(qm[:, ks], km[:, ks], _NT, preferred_element_type=f32)
            if 2 * m_half < c:
                p = jnp.where(same_blk, p, 0.0)
            att[h] = p if att[h] is None else att[h] + p
        m_half //= 2
    q_in = (gq * jnp.exp(lam)).astype(bf16)
    k_out = (gk * jnp.exp(lam[c - 1:c, :] - lam)).astype(bf16)
    for h in range(G_HEADS):
        ks = slice(h * G_DK, (h + 1) * G_DK)
        vs = slice(h * G_DV, (h + 1) * G_DV)
        v = z_ref[0, :, Z_GV + h * G_DV:Z_GV + (h + 1) * G_DV]
        vb = v.astype(bf16)
        diag = jnp.sum(gq[:, ks] * gk[:, ks], axis=1, keepdims=True)
        S_ = S_sc[h]
        o = (jnp.dot(att[h].astype(bf16), vb, preferred_element_type=f32) + diag * v
             + jnp.dot(q_in[:, ks], S_.astype(bf16), preferred_element_type=f32))
        S_sc[h] = (jnp.exp(lamT[h * G_DK:(h + 1) * G_DK, c - 1:c]) * S_
                   + lax.dot_general(k_out[:, ks], vb, _TN, preferred_element_type=f32))
        on = o * lax.rsqrt(jnp.mean(o * o, axis=1, keepdims=True) + EPS)
        gr = z_ref[0, :, Z_GR + h * G_DV:Z_GR + (h + 1) * G_DV]
        br_ref[0, :, BR_W + h * G_DV:BR_W + (h + 1) * G_DV] = on * gnorm[:, vs] * jax.nn.silu(gr)

    xp_sc[8:8 + c, :] = z_ref[0, :, Z_XBC:Z_XBC + S_XBC]
    conv = cb[...] + cw[0:1, :] * xp_sc[pl.ds(5, c), :]
    for jj in range(1, S_CONV):
        conv = conv + cw[jj:jj + 1, :] * xp_sc[pl.ds(5 + jj, c), :]
    new_buf = xp_sc[pl.ds(5 + valid, S_CONV - 1), :]
    xp_sc[5:8, :] = new_buf
    act = jax.nn.silu(conv)
    ys = []
    ssq = [None] * S_GROUPS
    for g in range(S_GROUPS):
        Bm = act[:, S_W + g * S_STATE:S_W + (g + 1) * S_STATE].astype(bf16)
        Cmat = act[:, S_W + S_GROUPS * S_STATE + g * S_STATE:S_W + S_GROUPS * S_STATE + (g + 1) * S_STATE].astype(bf16)
        cbm = lax.dot_general(Cmat, Bm, _NT, preferred_element_type=f32)
        for hg in range(S_HG):
            hh_ = g * S_HG + hg
            ps = slice(hh_ * S_HD, (hh_ + 1) * S_HD)
            xs = act[:, ps]
            dt_c = DT[:, SM_DT + hh_:SM_DT + hh_ + 1]
            dt_r = DTT[SM_DT + hh_:SM_DT + hh_ + 1, :]
            l_c = cum[:, SM_DT + hh_:SM_DT + hh_ + 1]
            l_r = cumT[SM_DT + hh_:SM_DT + hh_ + 1, :]
            decay = jnp.exp(jnp.where(causal, l_c - l_r, NEG_INF))
            w = (cbm * decay * dt_r).astype(bf16)
            hs_ = h_sc[hh_]
            y = (jnp.dot(w, xs.astype(bf16), preferred_element_type=f32)
                 + lax.dot_general(Cmat, hs_.astype(bf16), _NT, preferred_element_type=f32) * jnp.exp(l_c))
            l_last = l_c[c - 1:c, :]
            ws = jnp.exp(l_last - l_c) * dt_c
            h_sc[hh_] = (jnp.exp(l_last) * hs_
                         + lax.dot_general((ws * xs).astype(bf16), Bm, _TN, preferred_element_type=f32))
            yy = (y + drow[:, ps] * xs) * jax.nn.silu(z_ref[0, :, Z_SZ + hh_ * S_HD:Z_SZ + (hh_ + 1) * S_HD])
            ys.append(yy)
            sq = jnp.sum(yy * yy, axis=1, keepdims=True)
            ssq[g] = sq if ssq[g] is None else ssq[g] + sq
    for hh_ in range(S_HEADS):
        g = hh_ // S_HG
        ps = slice(hh_ * S_HD, (hh_ + 1) * S_HD)
        scale = lax.rsqrt(ssq[g] * (1.0 / (S_W // S_GROUPS)) + EPS)
        br_ref[0, :, 2 * BR_W + hh_ * S_HD:2 * BR_W + (hh_ + 1) * S_HD] = ys[hh_] * scale * snorm[:, ps]

    @pl.when(j == pl.num_programs(1) - 1)
    def _store_state():
        C1[0] = C_sc[...]
        n1[0] = n_sc[...]
        m1[0] = m_sc[...]
        S1[0] = S_sc[...]
        h1[0] = h_sc[...]
        buf1[0] = xp_sc[5:8, :]


def _mixer(z, states, W, c, valid):
    B, L, _ = z.shape
    nc = L // c
    C0, n0, m0, S0, h0, buf0 = states
    m0 = m0.reshape(B, 1, M_HEADS)

    def per_b(shape):
        nd = len(shape)
        return pl.BlockSpec((1,) + shape, lambda b, j: (b,) + (0,) * nd)

    def const(shape):
        nd = len(shape)
        return pl.BlockSpec(shape, lambda b, j: (0,) * nd)

    small_w = (W["bias_s"], W["alog"], W["mnorm"], W["gnorm"], W["snorm"], W["drow"],
               W["gup"], W["gab"], W["cw"], W["cb"])
    st_shapes = [(M_HEADS, M_HD, M_HD), (M_HEADS, M_HD), (M_HEADS, LANE),
                 (G_HEADS, G_DK, G_DV), (S_HEADS, S_HD, S_STATE), (S_CONV - 1, S_XBC)]
    out = pl.pallas_call(
        functools.partial(_mixer_body, c, valid),
        grid=(B, nc),
        in_specs=[pl.BlockSpec((1, c, Z_W), lambda b, j: (b, j, 0)),
                  per_b((M_HEADS, M_HD, M_HD)), per_b((M_HEADS, M_HD)), per_b((1, M_HEADS)),
                  per_b((G_HEADS, G_DK, G_DV)), per_b((S_HEADS, S_HD, S_STATE)), per_b((S_CONV - 1, S_XBC))]
                 + [const(w.shape) for w in small_w],
        out_specs=[pl.BlockSpec((1, c, N_BRANCH * BR_W), lambda b, j: (b, j, 0))]
                  + [per_b(s) for s in st_shapes],
        out_shape=[jax.ShapeDtypeStruct((B, L, N_BRANCH * BR_W), f32)]
                  + [jax.ShapeDtypeStruct((B,) + s, f32) for s in st_shapes],
        scratch_shapes=[pltpu.VMEM((M_HEADS, M_HD, M_HD), f32), pltpu.VMEM((M_HEADS, M_HD), f32),
                        pltpu.VMEM((M_HEADS, LANE), f32), pltpu.VMEM((G_HEADS, G_DK, G_DV), f32),
                        pltpu.VMEM((S_HEADS, S_HD, S_STATE), f32), pltpu.VMEM((c + 8, S_XBC), f32)],
        compiler_params=_cparams("parallel", "arbitrary"),
        name="mixer",
    )(z, C0, n0, m0, S0, h0, buf0, *small_w)
    br, C1, n1, m1, S1, h1, buf1 = out
    return br, (C1, n1, m1[:, :, 0], S1, h1, buf1)


def _layernorm(x, g, b):
    mu = jnp.mean(x, axis=1, keepdims=True)
    xc = x - mu
    return xc * lax.rsqrt(jnp.mean(xc * xc, axis=1, keepdims=True) + EPS) * g + b


def _merge_body(x_ref, br_ref, wg_ref, wb_ref, wo_ref, g_ref, b_ref, o_ref):
    x = x_ref[...]
    xb = x.astype(bf16)
    mixed = None
    for n in range(N_BRANCH):
        gate = jnp.dot(xb, wg_ref[:, n * D_MODEL:(n + 1) * D_MODEL], preferred_element_type=f32)
        proj = jnp.dot(br_ref[:, n * BR_W:(n + 1) * BR_W].astype(bf16), wb_ref[n], preferred_element_type=f32)
        t = jax.nn.sigmoid(gate) * proj
        mixed = t if mixed is None else mixed + t
    y = jnp.dot(mixed.astype(bf16), wo_ref[...], preferred_element_type=f32)
    o_ref[...] = _layernorm(DN_ALPHA * x + y, g_ref[...], b_ref[...])


def _merge(x, br, W, tm):
    T = x.shape[0]
    full = lambda a: pl.BlockSpec(a.shape, lambda i: (0,) * a.ndim)
    ws = (W["w_gate"], W["w_branch"], W["w_out"], W["ln1_g"], W["ln1_b"])
    return pl.pallas_call(
        _merge_body,
        grid=(T // tm,),
        in_specs=[pl.BlockSpec((tm, D_MODEL), lambda i: (i, 0)),
                  pl.BlockSpec((tm, N_BRANCH * BR_W), lambda i: (i, 0))] + [full(w) for w in ws],
        out_specs=pl.BlockSpec((tm, D_MODEL), lambda i: (i, 0)),
        out_shape=jax.ShapeDtypeStruct((T, D_MODEL), f32),
        compiler_params=_cparams("parallel"),
        name="merge",
    )(x, br, *ws)


ROUTE_TT = 128
_CAND_ROWS = [P_TOPK, 8] + [8] * (P_TOPK - 2)
N_CAND = sum(_CAND_ROWS)


def _route_body(q_ref, keys_ref, r2_ref, e2_ref, lim_ref, cw_ref, v_sc):
    tt = q_ref.shape[1]
    rowi = lax.broadcasted_iota(i32, (P_NKEYS, tt), 0)
    cr = lax.broadcasted_iota(i32, (N_CAND, 1), 0)
    cpos = jnp.where(cr < 24, cr, (2 + ((cr - 24) >> 3)) * P_TOPK + ((cr - 24) & 7))
    k1row = lax.broadcasted_iota(i32, (P_TOPK, 1), 0)

    def head(h, carry):
        ranks, es = [], []
        for jh in range(2):
            qh = q_ref[2 * h + jh].astype(bf16)
            sc = lax.dot_general(keys_ref[2 * h + jh], qh, _NT, preferred_element_type=f32)
            work = sc
            rank = jnp.full((P_NKEYS, tt), P_TOPK, i32)
            for k in range(P_TOPK):
                m = jnp.max(work, axis=0, keepdims=True)
                idx = jnp.min(jnp.where(work == m, rowi, P_NKEYS), axis=0, keepdims=True)
                hit = rowi == idx
                rank = jnp.where(hit, k, rank)
                work = jnp.where(hit, NEG_INF, work)
                v_sc[jh, k:k + 1, :] = m
            ranks.append(rank)
            es.append(jnp.exp(sc - v_sc[jh, 0:1, :]))
        v1 = v_sc[0]
        v2 = v_sc[1]
        pieces = []
        for k1, nrow in enumerate(_CAND_ROWS):
            pieces.append(v1[k1:k1 + 1, :] + v2[0:nrow, :])
        cand = jnp.concatenate(pieces, axis=0)
        cnt = jnp.zeros((P_TOPK, tt), f32)
        zsum = None
        top0 = None
        for r in range(P_TOPK):
            m = jnp.max(cand, axis=0, keepdims=True)
            ps = jnp.min(jnp.where(cand == m, cpos, 4 * P_TOPK * P_TOPK), axis=0, keepdims=True)
            cand = jnp.where(cpos == ps, NEG_INF, cand)
            cnt = cnt + jnp.where(k1row == (ps >> 4), 1.0, 0.0)
            if r == 0:
                top0 = m
                zsum = jnp.ones_like(m)
            else:
                zsum = zsum + jnp.exp(m - top0)
        lim = jnp.zeros((P_NKEYS, tt), f32)
        for k1 in range(P_TOPK):
            lim = jnp.where(ranks[0] == k1, cnt[k1:k1 + 1, :], lim)
        r2_ref[h] = ranks[1].astype(f32)
        e2_ref[h] = es[1]
        lim_ref[h] = lim
        cw_ref[h] = es[0] / zsum
        return carry

    lax.fori_loop(0, P_HEADS, head, 0)


def _route(q16, keys):
    G, T, _ = q16.shape
    tt = ROUTE_TT
    spec = pl.BlockSpec((P_HEADS, P_NKEYS, tt), lambda i: (0, 0, i))
    shp = jax.ShapeDtypeStruct((P_HEADS, P_NKEYS, T), f32)
    return pl.pallas_call(
        _route_body,
        grid=(T // tt,),
        in_specs=[pl.BlockSpec((G, tt, P_HALF), lambda i: (0, i, 0)),
                  pl.BlockSpec(keys.shape, lambda i: (0, 0, 0))],
        out_specs=[spec] * 4,
        out_shape=[shp] * 4,
        scratch_shapes=[pltpu.VMEM((2, P_TOPK, tt), f32)],
        compiler_params=_cparams("parallel"),
        name="peer_route",
    )(q16, keys)


PEER_NA = 8
PEER_EB = PEER_NA * P_NKEYS


def _peer_body(x_ref, u_ref, vt_ref, r2_ref, e2_ref, lim_ref, cw_ref, g_ref, b_ref, o_ref,
               xb_sc, acc_sc, w_sc):
    e = pl.program_id(1)

    @pl.when(e == 0)
    def _init():
        xb_sc[...] = x_ref[...].astype(bf16)
        acc_sc[...] = jnp.zeros_like(acc_sc)

    act = lax.dot_general(u_ref[...], xb_sc[...], _NT, preferred_element_type=f32)
    for al in range(PEER_NA):
        gate = None
        for h in range(P_HEADS):
            t = jnp.where(r2_ref[h] < lim_ref[h, al:al + 1, :], e2_ref[h], 0.0) * cw_ref[h, al:al + 1, :]
            gate = t if gate is None else gate + t
        a = act[al * P_NKEYS:(al + 1) * P_NKEYS, :]
        gelu = 0.5 * a * (1.0 + lax.erf(a * (2.0 ** -0.5)))
        w_sc[al * P_NKEYS:(al + 1) * P_NKEYS, :] = (gelu * gate).astype(bf16)
    acc_sc[...] += jnp.dot(vt_ref[...], w_sc[...], preferred_element_type=f32)

    @pl.when(e == pl.num_programs(1) - 1)
    def _finish():
        y = acc_sc[...].T
        o_ref[...] = _layernorm(DN_ALPHA * x_ref[...] + y, g_ref[...], b_ref[...])


def _peer(x, route, W, tt):
    T = x.shape[0]
    r2, e2, lim, cw = route
    ne = P_EXPERTS // PEER_EB
    tok = pl.BlockSpec((P_HEADS, P_NKEYS, tt), lambda i, e: (0, 0, i))
    per_a = pl.BlockSpec((P_HEADS, PEER_NA, tt), lambda i, e: (0, e, i))
    vec = pl.BlockSpec((1, D_MODEL), lambda i, e: (0, 0))
    return pl.pallas_call(
        _peer_body,
        grid=(T // tt, ne),
        in_specs=[pl.BlockSpec((tt, D_MODEL), lambda i, e: (i, 0)),
                  pl.BlockSpec((PEER_EB, D_MODEL), lambda i, e: (e, 0)),
                  pl.BlockSpec((D_MODEL, PEER_EB), lambda i, e: (0, e)),
                  tok, tok, per_a, per_a, vec, vec],
        out_specs=pl.BlockSpec((tt, D_MODEL), lambda i, e: (i, 0)),
        out_shape=jax.ShapeDtypeStruct((T, D_MODEL), f32),
        scratch_shapes=[pltpu.VMEM((tt, D_MODEL), bf16), pltpu.VMEM((D_MODEL, tt), f32),
                        pltpu.VMEM((PEER_EB, tt), bf16)],
        compiler_params=_cparams("parallel", "arbitrary"),
        name="peer_experts",
    )(x, W["p_u"], W["p_vt"], r2, e2, lim, cw, W["ln2_g"], W["ln2_b"])


_IN_SIZES = (512, 512, 512, 512, 4, 4, 256, 256, 512, 512, 16, 512, 768, 8, 3072)


def _prep_layer(w_in, m_i_bias, m_f_bias, m_norm, g_a_up, g_a_bias, g_norm,
                s_conv_w, s_conv_b, s_dt_bias, s_A_log, s_D, s_norm,
                w_branch, w_out, ln1_g, ln1_b, p_wq, p_keys, p_u, p_v, ln2_g, ln2_b):
    offs = [0]
    for s in _IN_SIZES:
        offs.append(offs[-1] + s)
    colw = lambda i: w_in[:, offs[i]:offs[i + 1]]
    (mq, mk, mv, mo, mi, mf, gq, gk, gv, gr, ga, sz, sxbc, sdt, gate) = [colw(i) for i in range(15)]
    pad = jnp.zeros((D_MODEL, LANE - (SM_DT + S_HEADS)), w_in.dtype)
    w1 = jnp.concatenate([mq, mk, mv, mo, gq, gk, gv, gr, sz, sxbc, mi, mf, ga, sdt, pad], axis=1)
    zl = lambda n: jnp.zeros((n,), f32)
    row = lambda a: a.astype(f32).reshape(1, -1)
    return dict(
        w1=w1.astype(bf16),
        w_gate=gate.astype(bf16),
        bias_s=row(jnp.concatenate([m_i_bias, m_f_bias, zl(G_RANK), s_dt_bias, zl(LANE - SM_DT - S_HEADS)])),
        alog=row(jnp.concatenate([zl(SM_DT), s_A_log, zl(LANE - SM_DT - S_HEADS)])),
        mnorm=row(m_norm), gnorm=row(g_norm), snorm=row(s_norm),
        drow=row(jnp.repeat(s_D, S_HD)),
        gup=jnp.zeros((LANE, G_KW), f32).at[SM_GA:SM_GA + G_RANK].set(g_a_up).astype(bf16),
        gab=row(g_a_bias),
        cw=s_conv_w.astype(f32), cb=row(s_conv_b),
        w_branch=w_branch.astype(bf16), w_out=w_out.astype(bf16),
        ln1_g=row(ln1_g), ln1_b=row(ln1_b),
        p_wq=p_wq.astype(bf16),
        p_keys=p_keys.astype(bf16).reshape(2 * P_HEADS, P_NKEYS, P_HALF),
        p_u=p_u.astype(bf16), p_vt=p_v.astype(bf16).T,
        ln2_g=row(ln2_g), ln2_b=row(ln2_b),
    )


def _run_layer(x, states, W, c, valid, tm, peer_tt):
    B, L, _ = x.shape
    nchunks = L // valid
    Lp = nchunks * c
    if c != valid:
        xp = jnp.pad(x.reshape(B, nchunks, valid, D_MODEL), ((0, 0), (0, 0), (0, c - valid), (0, 0)))
        xp = xp.reshape(B * Lp, D_MODEL)
    else:
        xp = x.reshape(B * L, D_MODEL)
    z = _matmul(xp, W["w1"], tm).reshape(B, Lp, Z_W)
    br, new_states = _mixer(z, states, W, c, valid)
    if c != valid:
        br = br.reshape(B, nchunks, c, N_BRANCH * BR_W)[:, :, :valid]
    xt = x.reshape(B * L, D_MODEL)
    h = _merge(xt, br.reshape(B * L, N_BRANCH * BR_W), W, tm)
    q16 = _wq_matmul(h, W["p_wq"], tm)
    route = _route(q16, W["p_keys"])
    y = _peer(h, route, W, peer_tt)
    return y.reshape(B, L, D_MODEL), new_states


def _zero_states(b):
    return (jnp.zeros((b, M_HEADS, M_HD, M_HD), f32), jnp.zeros((b, M_HEADS, M_HD), f32),
            jnp.zeros((b, M_HEADS), f32), jnp.zeros((b, G_HEADS, G_DK, G_DV), f32),
            jnp.zeros((b, S_HEADS, S_HD, S_STATE), f32), jnp.zeros((b, S_CONV - 1, S_XBC), f32))


PROMPT_CHUNK = 128
SAMPLE_CHUNK = 16


def kernel(x_prompt, x_sample, state_mlstm_C, state_mlstm_n, state_mlstm_m, state_gla_S, state_ssm_h, state_conv, w_in, m_i_bias, m_f_bias, m_norm, g_a_up, g_a_bias, g_norm, s_conv_w, s_conv_b, s_dt_bias, s_A_log, s_D, s_norm, w_branch, w_out, ln1_g, ln1_b, p_wq, p_keys, p_u, p_v, ln2_g, ln2_b):
    weights = (w_in, m_i_bias, m_f_bias, m_norm, g_a_up, g_a_bias, g_norm,
               s_conv_w, s_conv_b, s_dt_bias, s_A_log, s_D, s_norm,
               w_branch, w_out, ln1_g, ln1_b, p_wq, p_keys, p_u, p_v, ln2_g, ln2_b)
    in_states = (state_mlstm_C, state_mlstm_n, state_mlstm_m, state_gla_S, state_ssm_h, state_conv)
    hp, hs = x_prompt, x_sample
    ls = x_sample.shape[1]
    new_p = [[] for _ in range(6)]
    new_s = [[] for _ in range(6)]
    for l in range(DEPTH):
        W = _prep_layer(*[w[l] for w in weights])
        hp, sp = _run_layer(hp, _zero_states(hp.shape[0]), W, PROMPT_CHUNK, PROMPT_CHUNK, 256, 512)
        hs, ss = _run_layer(hs, tuple(s[l] for s in in_states), W, SAMPLE_CHUNK, ls, 256, 512)
        for j in range(6):
            new_p[j].append(sp[j])
            new_s[j].append(ss[j])
    P = [jnp.stack(a, axis=0) for a in new_p]
    S = [jnp.stack(a, axis=0) for a in new_s]
    return (hp, hs, P[0], P[1], P[2], P[3], P[4], P[5], S[0], S[1], S[2], S[3], S[4], S[5])
```

```python
import functools
import math

import numpy as np
import jax
import jax.numpy as jnp
from jax import lax
from jax.experimental import pallas as pl
from jax.experimental.pallas import tpu as pltpu

f32, bf16, i32 = jnp.float32, jnp.bfloat16, jnp.int32
NEG_INF = float("-inf")

D_MODEL = 1024
DEPTH = 2
BR_W = 512
N_BRANCH = 3
M_HEADS, M_HD = 4, 128
G_HEADS, G_DK, G_DV = 4, 64, 128
G_KW, G_VW = G_HEADS * G_DK, G_HEADS * G_DV
G_RANK = 16
G_TAU = 16.0
S_HD, S_HEADS, S_GROUPS, S_STATE, S_CONV = 64, 8, 2, 64, 4
S_HG = S_HEADS // S_GROUPS
S_W = S_HEADS * S_HD
S_XBC = S_W + 2 * S_GROUPS * S_STATE
P_HEADS, P_NKEYS, P_HALF, P_TOPK = 8, 128, 128, 16
P_EXPERTS = P_NKEYS * P_NKEYS
DN_ALPHA = (2.0 * DEPTH) ** 0.25
EPS = 1e-5

LANE = 128

Z_MQ, Z_MK, Z_MV, Z_MO = 0, 512, 1024, 1536
Z_GQ, Z_GK, Z_GV, Z_GR = 2048, 2304, 2560, 3072
Z_SZ, Z_XBC, Z_SMALL = 3584, 4096, 4864
Z_W = Z_SMALL + LANE
SM_MI, SM_MF, SM_GA, SM_DT = 0, 4, 8, 24

VMEM_LIMIT = 56 * 1024 * 1024

_NT = (((1,), (1,)), ((), ()))
_TN = (((0,), (0,)), ((), ()))
_HI = lax.Precision.HIGHEST


def _cparams(*sem):
    return pltpu.CompilerParams(dimension_semantics=sem, vmem_limit_bytes=VMEM_LIMIT)


def _mm_body(x_ref, w_ref, o_ref):
    o_ref[...] = jnp.dot(x_ref[...].astype(bf16), w_ref[...], preferred_element_type=f32)


def _matmul(x, w, tm):
    T, K = x.shape
    N = w.shape[1]
    return pl.pallas_call(
        _mm_body,
        grid=(T // tm,),
        in_specs=[pl.BlockSpec((tm, K), lambda i: (i, 0)), pl.BlockSpec((K, N), lambda i: (0, 0))],
        out_specs=pl.BlockSpec((tm, N), lambda i: (i, 0)),
        out_shape=jax.ShapeDtypeStruct((T, N), f32),
        compiler_params=_cparams("parallel"),
        name="in_proj",
    )(x, w)


def _wq_body(x_ref, w_ref, o_ref):
    r = jnp.dot(x_ref[...].astype(bf16), w_ref[...], preferred_element_type=f32)
    for g in range(2 * P_HEADS):
        o_ref[g] = r[:, g * P_HALF:(g + 1) * P_HALF]


def _wq_matmul(x, w, tm):
    T, K = x.shape
    G = 2 * P_HEADS
    return pl.pallas_call(
        _wq_body,
        grid=(T // tm,),
        in_specs=[pl.BlockSpec((tm, K), lambda i: (i, 0)), pl.BlockSpec((K, G * P_HALF), lambda i: (0, 0))],
        out_specs=pl.BlockSpec((G, tm, P_HALF), lambda i: (0, i, 0)),
        out_shape=jax.ShapeDtypeStruct((G, T, P_HALF), f32),
        compiler_params=_cparams("parallel"),
        name="peer_query",
    )(x, w)


def _mixer_body(c, valid,
                z_ref, C0, n0, m0, S0, h0, buf0,
                bias_s, alog, mnorm, gnorm, snorm, drow, gup, gab, cw, cb,
                br_ref, C1, n1, m1, S1, h1, buf1,
                C_sc, n_sc, m_sc, S_sc, h_sc, xp_sc):
    j = pl.program_id(1)

    @pl.when(j == 0)
    def _load_state():
        C_sc[...] = C0[0]
        n_sc[...] = n0[0]
        for h in range(M_HEADS):
            m_sc[h:h + 1, :] = jnp.broadcast_to(m0[0, :, h:h + 1], (1, LANE))
        S_sc[...] = S0[0]
        h_sc[...] = h0[0]
        xp_sc[5:8, :] = buf0[0]

    row = lax.broadcasted_iota(i32, (c, c), 0)
    col = lax.broadcasted_iota(i32, (c, c), 1)
    causal = row >= col
    tri = causal.astype(f32)
    lane = lax.broadcasted_iota(i32, (1, LANE), 1)
    tcol = lax.broadcasted_iota(i32, (c, 1), 0)
    tvalid = tcol < valid

    zs = z_ref[0, :, Z_SMALL:Z_SMALL + LANE]
    small = zs + bias_s[...]
    is_lf = (lane >> 2) == (SM_MF >> 2)
    is_dt = (lane >> 3) == (SM_DT >> 3)
    LI = jnp.where(tvalid, small, NEG_INF)
    LF = jnp.where(tvalid, jnp.where(is_lf, jax.nn.log_sigmoid(small), 0.0), 0.0)
    DT = jnp.where(tvalid, jnp.where(is_dt, jax.nn.softplus(small), 0.0), 0.0)
    a_row = jnp.where(is_dt, -jnp.exp(alog[...]), 0.0)
    cum = jnp.dot(tri, LF + DT * a_row, precision=_HI, preferred_element_type=f32)
    cumT = cum.T
    LIT = LI.T
    DTT = DT.T

    for h in range(M_HEADS):
        sl = slice(h * M_HD, (h + 1) * M_HD)
        q = z_ref[0, :, Z_MQ + h * M_HD:Z_MQ + (h + 1) * M_HD] * (M_HD ** -0.5)
        k = z_ref[0, :, Z_MK + h * M_HD:Z_MK + (h + 1) * M_HD]
        v = z_ref[0, :, Z_MV + h * M_HD:Z_MV + (h + 1) * M_HD]
        qb, kb, vb = q.astype(bf16), k.astype(bf16), v.astype(bf16)
        b_c = cum[:, SM_MF + h:SM_MF + h + 1]
        b_r = cumT[SM_MF + h:SM_MF + h + 1, :]
        li_c = LI[:, SM_MI + h:SM_MI + h + 1]
        li_r = LIT[SM_MI + h:SM_MI + h + 1, :]
        m_prev = m_sc[h:h + 1, 0:1]
        a = b_c + m_prev
        d = jnp.where(causal, b_c - b_r + li_r, NEG_INF)
        m_t = jnp.maximum(a, jnp.max(d, axis=1, keepdims=True))
        s = lax.dot_general(qb, kb, _NT, preferred_element_type=f32) * jnp.exp(d - m_t)
        e_in = jnp.exp(a - m_t)
        Cm_ = C_sc[h]
        num = (jnp.dot(s.astype(bf16), vb, preferred_element_type=f32)
               + e_in * lax.dot_general(qb, Cm_.astype(bf16), _NT, preferred_element_type=f32))
        n_row = n_sc[h:h + 1, :]
        den = jnp.sum(s, axis=1, keepdims=True) + e_in * jnp.sum(q * n_row, axis=1, keepdims=True)
        hh = num / jnp.maximum(jnp.abs(den), jnp.exp(-m_t))
        b_last = b_c[c - 1:c, :]
        g_c = b_last - b_c + li_c
        m_new = jnp.maximum(b_last + m_prev, jnp.max(g_c, axis=0, keepdims=True))
        e_c = jnp.exp(b_last + m_prev - m_new)
        wg = jnp.exp(g_c - m_new)
        C_sc[h] = e_c * Cm_ + lax.dot_general((wg * v).astype(bf16), kb, _TN, preferred_element_type=f32)
        n_sc[h:h + 1, :] = e_c * n_row + jnp.sum(wg * k, axis=0, keepdims=True)
        m_sc[h:h + 1, :] = jnp.broadcast_to(m_new, (1, LANE))
        mu = jnp.mean(hh, axis=1, keepdims=True)
        xc = hh - mu
        hn = xc * lax.rsqrt(jnp.mean(xc * xc, axis=1, keepdims=True) + EPS)
        mo = z_ref[0, :, Z_MO + h * M_HD:Z_MO + (h + 1) * M_HD]
        br_ref[0, :, sl] = hn * mnorm[:, sl] * jax.nn.sigmoid(mo)

    ga = jnp.dot(zs.astype(bf16), gup[...], preferred_element_type=f32) + gab[...]
    log_a = jnp.where(tvalid, jax.nn.log_sigmoid(ga) * (1.0 / G_TAU), 0.0)
    lam = jnp.dot(tri, log_a, precision=_HI, preferred_element_type=f32)
    lamT = lam.T
    gq = z_ref[0, :, Z_GQ:Z_GQ + G_KW] * (G_DK ** -0.5)
    gk = z_ref[0, :, Z_GK:Z_GK + G_KW]
    att = [None] * G_HEADS
    m_half = c // 2
    while m_half >= 1:
        sh = int(math.log2(2 * m_half))
        upper = (tcol & (2 * m_half - 1)) >= m_half
        sel = (col == ((row >> sh) << sh) + (m_half - 1)).astype(f32)
        beta = jnp.dot(sel, lam, precision=_HI, preferred_element_type=f32)
        qm = (gq * jnp.exp(jnp.where(upper, lam - beta, NEG_INF))).astype(bf16)
        km = (gk * jnp.exp(jnp.where(upper, NEG_INF, beta - lam))).astype(bf16)
        same_blk = (row >> sh) == (col >> sh)
        for h in range(G_HEADS):
            ks = slice(h * G_DK, (h + 1) * G_DK)
            p = lax.dot_general(qm[:, ks], km[:, ks], _NT, preferred_element_type=f32)
            if 2 * m_half < c:
                p = jnp.where(same_blk, p, 0.0)
            att[h] = p if att[h] is None else att[h] + p
        m_half //= 2
    q_in = (gq * jnp.exp(lam)).astype(bf16)
    k_out = (gk * jnp.exp(lam[c - 1:c, :] - lam)).astype(bf16)
    for h in range(G_HEADS):
        ks = slice(h * G_DK, (h + 1) * G_DK)
        vs = slice(h * G_DV, (h + 1) * G_DV)
        v = z_ref[0, :, Z_GV + h * G_DV:Z_GV + (h + 1) * G_DV]
        vb = v.astype(bf16)
        diag = jnp.sum(gq[:, ks] * gk[:, ks], axis=1, keepdims=True)
        S_ = S_sc[h]
        o = (jnp.dot(att[h].astype(bf16), vb, preferred_element_type=f32) + diag * v
             + jnp.dot(q_in[:, ks], S_.astype(bf16), preferred_element_type=f32))
        S_sc[h] = (jnp.exp(lamT[h * G_DK:(h + 1) * G_DK, c - 1:c]) * S_
                   + lax.dot_general(k_out[:, ks], vb, _TN, preferred_element_type=f32))
        on = o * lax.rsqrt(jnp.mean(o * o, axis=1, keepdims=True) + EPS)
        gr = z_ref[0, :, Z_GR + h * G_DV:Z_GR + (h + 1) * G_DV]
        br_ref[0, :, BR_W + h * G_DV:BR_W + (h + 1) * G_DV] = on * gnorm[:, vs] * jax.nn.silu(gr)

    xp_sc[8:8 + c, :] = z_ref[0, :, Z_XBC:Z_XBC + S_XBC]
    conv = cb[...] + cw[0:1, :] * xp_sc[pl.ds(5, c), :]
    for jj in range(1, S_CONV):
        conv = conv + cw[jj:jj + 1, :] * xp_sc[pl.ds(5 + jj, c), :]
    new_buf = xp_sc[pl.ds(5 + valid, S_CONV - 1), :]
    xp_sc[5:8, :] = new_buf
    act = jax.nn.silu(conv)
    ys = []
    ssq = [None] * S_GROUPS
    for g in range(S_GROUPS):
        Bm = act[:, S_W + g * S_STATE:S_W + (g + 1) * S_STATE].astype(bf16)
        Cmat = act[:, S_W + S_GROUPS * S_STATE + g * S_STATE:S_W + S_GROUPS * S_STATE + (g + 1) * S_STATE].astype(bf16)
        cbm = lax.dot_general(Cmat, Bm, _NT, preferred_element_type=f32)
        for hg in range(S_HG):
            hh_ = g * S_HG + hg
            ps = slice(hh_ * S_HD, (hh_ + 1) * S_HD)
            xs = act[:, ps]
            dt_c = DT[:, SM_DT + hh_:SM_DT + hh_ + 1]
            dt_r = DTT[SM_DT + hh_:SM_DT + hh_ + 1, :]
            l_c = cum[:, SM_DT + hh_:SM_DT + hh_ + 1]
            l_r = cumT[SM_DT + hh_:SM_DT + hh_ + 1, :]
            decay = jnp.exp(jnp.where(causal, l_c - l_r, NEG_INF))
            w = (cbm * decay * dt_r).astype(bf16)
            hs_ = h_sc[hh_]
            y = (jnp.dot(w, xs.astype(bf16), preferred_element_type=f32)
                 + lax.dot_general(Cmat, hs_.astype(bf16), _NT, preferred_element_type=f32) * jnp.exp(l_c))
            l_last = l_c[c - 1:c, :]
            ws = jnp.exp(l_last - l_c) * dt_c
            h_sc[hh_] = (jnp.exp(l_last) * hs_
                         + lax.dot_general((ws * xs).astype(bf16), Bm, _TN, preferred_element_type=f32))
            yy = (y + drow[:, ps] * xs) * jax.nn.silu(z_ref[0, :, Z_SZ + hh_ * S_HD:Z_SZ + (hh_ + 1) * S_HD])
            ys.append(yy)
            sq = jnp.sum(yy * yy, axis=1, keepdims=True)
            ssq[g] = sq if ssq[g] is None else ssq[g] + sq
    for hh_ in range(S_HEADS):
        g = hh_ // S_HG
        ps = slice(hh_ * S_HD, (hh_ + 1) * S_HD)
        scale = lax.rsqrt(ssq[g] * (1.0 / (S_W // S_GROUPS)) + EPS)
        br_ref[0, :, 2 * BR_W + hh_ * S_HD:2 * BR_W + (hh_ + 1) * S_HD] = ys[hh_] * scale * snorm[:, ps]

    @pl.when(j == pl.num_programs(1) - 1)
    def _store_state():
        C1[0] = C_sc[...]
        n1[0] = n_sc[...]
        m1[0] = m_sc[...]
        S1[0] = S_sc[...]
        h1[0] = h_sc[...]
        buf1[0] = xp_sc[5:8, :]


def _mixer(z, states, W, c, valid):
    B, L, _ = z.shape
    nc = L // c
    C0, n0, m0, S0, h0, buf0 = states
    m0 = m0.reshape(B, 1, M_HEADS)

    def per_b(shape):
        nd = len(shape)
        return pl.BlockSpec((1,) + shape, lambda b, j: (b,) + (0,) * nd)

    def const(shape):
        nd = len(shape)
        return pl.BlockSpec(shape, lambda b, j: (0,) * nd)

    small_w = (W["bias_s"], W["alog"], W["mnorm"], W["gnorm"], W["snorm"], W["drow"],
               W["gup"], W["gab"], W["cw"], W["cb"])
    st_shapes = [(M_HEADS, M_HD, M_HD), (M_HEADS, M_HD), (M_HEADS, LANE),
                 (G_HEADS, G_DK, G_DV), (S_HEADS, S_HD, S_STATE), (S_CONV - 1, S_XBC)]
    out = pl.pallas_call(
        functools.partial(_mixer_body, c, valid),
        grid=(B, nc),
        in_specs=[pl.BlockSpec((1, c, Z_W), lambda b, j: (b, j, 0)),
                  per_b((M_HEADS, M_HD, M_HD)), per_b((M_HEADS, M_HD)), per_b((1, M_HEADS)),
                  per_b((G_HEADS, G_DK, G_DV)), per_b((S_HEADS, S_HD, S_STATE)), per_b((S_CONV - 1, S_XBC))]
                 + [const(w.shape) for w in small_w],
        out_specs=[pl.BlockSpec((1, c, N_BRANCH * BR_W), lambda b, j: (b, j, 0))]
                  + [per_b(s) for s in st_shapes],
        out_shape=[jax.ShapeDtypeStruct((B, L, N_BRANCH * BR_W), f32)]
                  + [jax.ShapeDtypeStruct((B,) + s, f32) for s in st_shapes],
        scratch_shapes=[pltpu.VMEM((M_HEADS, M_HD, M_HD), f32), pltpu.VMEM((M_HEADS, M_HD), f32),
                        pltpu.VMEM((M_HEADS, LANE), f32), pltpu.VMEM((G_HEADS, G_DK, G_DV), f32),
                        pltpu.VMEM((S_HEADS, S_HD, S_STATE), f32), pltpu.VMEM((c + 8, S_XBC), f32)],
        compiler_params=_cparams("parallel", "arbitrary"),
        name="mixer",
    )(z, C0, n0, m0, S0, h0, buf0, *small_w)
    br, C1, n1, m1, S1, h1, buf1 = out
    return br, (C1, n1, m1[:, :, 0], S1, h1, buf1)


def _layernorm(x, g, b):
    mu = jnp.mean(x, axis=1, keepdims=True)
    xc = x - mu
    return xc * lax.rsqrt(jnp.mean(xc * xc, axis=1, keepdims=True) + EPS) * g + b


def _merge_body(x_ref, br_ref, wg_ref, wb_ref, wo_ref, g_ref, b_ref, o_ref):
    x = x_ref[...]
    xb = x.astype(bf16)
    mixed = None
    for n in range(N_BRANCH):
        gate = jnp.dot(xb, wg_ref[:, n * D_MODEL:(n + 1) * D_MODEL], preferred_element_type=f32)
        proj = jnp.dot(br_ref[:, n * BR_W:(n + 1) * BR_W].astype(bf16), wb_ref[n], preferred_element_type=f32)
        t = jax.nn.sigmoid(gate) * proj
        mixed = t if mixed is None else mixed + t
    y = jnp.dot(mixed.astype(bf16), wo_ref[...], preferred_element_type=f32)
    o_ref[...] = _layernorm(DN_ALPHA * x + y, g_ref[...], b_ref[...])


def _merge(x, br, W, tm):
    T = x.shape[0]
    full = lambda a: pl.BlockSpec(a.shape, lambda i: (0,) * a.ndim)
    ws = (W["w_gate"], W["w_branch"], W["w_out"], W["ln1_g"], W["ln1_b"])
    return pl.pallas_call(
        _merge_body,
        grid=(T // tm,),
        in_specs=[pl.BlockSpec((tm, D_MODEL), lambda i: (i, 0)),
                  pl.BlockSpec((tm, N_BRANCH * BR_W), lambda i: (i, 0))] + [full(w) for w in ws],
        out_specs=pl.BlockSpec((tm, D_MODEL), lambda i: (i, 0)),
        out_shape=jax.ShapeDtypeStruct((T, D_MODEL), f32),
        compiler_params=_cparams("parallel"),
        name="merge",
    )(x, br, *ws)


ROUTE_TT = 128
_STAIR = [(k1, k2) for k1 in range(P_TOPK) for k2 in range(P_TOPK // (k1 + 1))]
N_CAND = -(-len(_STAIR) // 8) * 8


def _cand_tables():
    p12 = np.zeros((N_CAND, 2 * P_TOPK), np.float32)
    pos = np.full((N_CAND, LANE), 4.0 * P_TOPK * P_TOPK, np.float32)
    bias = np.full((N_CAND, LANE), NEG_INF, np.float32)
    for r, (k1, k2) in enumerate(_STAIR):
        p12[r, k1] = 1.0
        p12[r, P_TOPK + k2] = 1.0
        pos[r, :] = k1 * P_TOPK + k2
        bias[r, :] = 0.0
    return jnp.asarray(p12), jnp.asarray(pos), jnp.asarray(bias)


def _route_body(q_ref, keys_ref, p12_ref, cpos_ref, cbias_ref, r2_ref, e2_ref, lim_ref, cw_ref, sc_sc, v_sc):
    tt = q_ref.shape[1]
    rowf = lax.broadcasted_iota(i32, (P_NKEYS, tt), 0).astype(f32)
    k1row = lax.broadcasted_iota(i32, (P_TOPK, 1), 0).astype(f32)
    cpos = cpos_ref[...]

    for g in range(2 * P_HEADS):
        sc_sc[g] = lax.dot_general(keys_ref[g], q_ref[g].astype(bf16), _NT, preferred_element_type=f32)

    A = B = C = None
    for step in range(P_HEADS + 2):
        if step < P_HEADS:
            A = dict(h=step, work=sc_sc[pl.ds(2 * step, 2)],
                     rank=jnp.full((2, P_NKEYS, tt), float(P_TOPK), f32))
        else:
            A = None
        for k in range(P_TOPK):
            if A is not None:
                work = A["work"]
                m = jnp.max(work, axis=1, keepdims=True)
                idx = jnp.min(jnp.where(work == m, rowf[None], float(P_NKEYS)), axis=1, keepdims=True)
                hit = rowf[None] == idx
                A["rank"] = jnp.where(hit, float(k), A["rank"])
                A["work"] = jnp.where(hit, NEG_INF, work)
                v_sc[A["h"] % 2, 0, k:k + 1, :] = m[0]
                v_sc[A["h"] % 2, 1, k:k + 1, :] = m[1]
            if B is not None:
                cand = B["cand"]
                m = jnp.max(cand, axis=0, keepdims=True)
                ps = jnp.min(jnp.where(cand == m, cpos, 4.0 * P_TOPK * P_TOPK), axis=0, keepdims=True)
                B["cand"] = jnp.where(cpos == ps, NEG_INF, cand)
                B["cnt"] = B["cnt"] + jnp.where(k1row == jnp.floor(ps * (1.0 / P_TOPK)), 1.0, 0.0)
                if k == 0:
                    B["top0"] = m
                    B["zsum"] = jnp.ones_like(m)
                else:
                    B["zsum"] = B["zsum"] + jnp.exp(m - B["top0"])
            if C is not None:
                C["lim"] = jnp.where(C["rank1"] == float(k), C["cnt"][k:k + 1, :], C["lim"])
        if C is not None:
            h = C["h"]
            lim_ref[h] = C["lim"]
            cw_ref[h] = jnp.exp(sc_sc[2 * h] - C["top1"]) / C["zsum"]
        C = None
        if B is not None:
            C = dict(h=B["h"], rank1=B["rank1"], cnt=B["cnt"], zsum=B["zsum"], top1=B["top1"],
                     lim=jnp.zeros((P_NKEYS, tt), f32))
        B = None
        if A is not None:
            h = A["h"]
            v1 = v_sc[h % 2, 0]
            v2 = v_sc[h % 2, 1]
            r2_ref[h] = A["rank"][1].astype(bf16)
            e2_ref[h] = jnp.exp(sc_sc[2 * h + 1] - v2[0:1, :]).astype(bf16)
            cand = (jnp.dot(p12_ref[:, 0:P_TOPK], v1, precision=_HI, preferred_element_type=f32)
                    + jnp.dot(p12_ref[:, P_TOPK:2 * P_TOPK], v2, precision=_HI, preferred_element_type=f32)
                    + cbias_ref[...])
            B = dict(h=h, cand=cand, cnt=jnp.zeros((P_TOPK, tt), f32), rank1=A["rank"][0], top1=v1[0:1, :])


def _route(q16, keys):
    G, T, _ = q16.shape
    tt = ROUTE_TT
    tabs = _cand_tables()
    spec = pl.BlockSpec((P_HEADS, P_NKEYS, tt), lambda i: (0, 0, i))
    full = lambda a: pl.BlockSpec(a.shape, lambda i: (0,) * a.ndim)
    return pl.pallas_call(
        _route_body,
        grid=(T // tt,),
        in_specs=[pl.BlockSpec((G, tt, P_HALF), lambda i: (0, i, 0)), full(keys)] + [full(t) for t in tabs],
        out_specs=[spec] * 4,
        out_shape=[jax.ShapeDtypeStruct((P_HEADS, P_NKEYS, T), d) for d in (bf16, bf16, f32, f32)],
        scratch_shapes=[pltpu.VMEM((2 * P_HEADS, P_NKEYS, tt), f32), pltpu.VMEM((2, 2, P_TOPK, tt), f32)],
        compiler_params=_cparams("parallel"),
        name="peer_route",
    )(q16, keys, *tabs)


PEER_NA = 8
PEER_EB = PEER_NA * P_NKEYS
BF16_ROWS = 16


def _peer_body(x_ref, u_ref, vt_ref, r2_ref, e2_ref, lim_ref, cw_ref, g_ref, b_ref, o_ref,
               xb_sc, acc_sc, w_sc):
    e = pl.program_id(1)
    tt = x_ref.shape[0]
    grp = (P_NKEYS // BF16_ROWS, BF16_ROWS, tt)

    @pl.when(e == 0)
    def _init():
        xb_sc[...] = x_ref[...].astype(bf16)
        acc_sc[...] = jnp.zeros_like(acc_sc)

    act = lax.dot_general(u_ref[...], xb_sc[...], _NT, preferred_element_type=f32)
    zero = jnp.zeros((), bf16)
    for al in range(PEER_NA):
        gate = None
        for h in range(P_HEADS):
            lim = jnp.broadcast_to(lim_ref[h, al:al + 1, :], (BF16_ROWS, tt)).astype(bf16)
            cw = jnp.broadcast_to(cw_ref[h, al:al + 1, :], (BF16_ROWS, tt)).astype(bf16)
            r2 = r2_ref[h].reshape(grp)
            e2 = e2_ref[h].reshape(grp)
            t = jnp.where(r2 < lim[None], e2, zero) * cw[None]
            gate = t if gate is None else gate + t
        a = act[al * P_NKEYS:(al + 1) * P_NKEYS, :]
        gelu = (0.5 * a * (1.0 + lax.erf(a * (2.0 ** -0.5)))).astype(bf16)
        w_sc[al * P_NKEYS:(al + 1) * P_NKEYS, :] = (gelu.reshape(grp) * gate).reshape(P_NKEYS, tt)
    acc_sc[...] += jnp.dot(vt_ref[...], w_sc[...], preferred_element_type=f32)

    @pl.when(e == pl.num_programs(1) - 1)
    def _finish():
        y = acc_sc[...].T
        o_ref[...] = _layernorm(DN_ALPHA * x_ref[...] + y, g_ref[...], b_ref[...])


def _peer(x, route, W, tt):
    T = x.shape[0]
    r2, e2, lim, cw = route
    ne = P_EXPERTS // PEER_EB
    tok = pl.BlockSpec((P_HEADS, P_NKEYS, tt), lambda i, e: (0, 0, i))
    per_a = pl.BlockSpec((P_HEADS, PEER_NA, tt), lambda i, e: (0, e, i))
    vec = pl.BlockSpec((1, D_MODEL), lambda i, e: (0, 0))
    return pl.pallas_call(
        _peer_body,
        grid=(T // tt, ne),
        in_specs=[pl.BlockSpec((tt, D_MODEL), lambda i, e: (i, 0)),
                  pl.BlockSpec((PEER_EB, D_MODEL), lambda i, e: (e, 0)),
                  pl.BlockSpec((D_MODEL, PEER_EB), lambda i, e: (0, e)),
                  tok, tok, per_a, per_a, vec, vec],
        out_specs=pl.BlockSpec((tt, D_MODEL), lambda i, e: (i, 0)),
        out_shape=jax.ShapeDtypeStruct((T, D_MODEL), f32),
        scratch_shapes=[pltpu.VMEM((tt, D_MODEL), bf16), pltpu.VMEM((D_MODEL, tt), f32),
                        pltpu.VMEM((PEER_EB, tt), bf16)],
        compiler_params=_cparams("parallel", "arbitrary"),
        name="peer_experts",
    )(x, W["p_u"], W["p_vt"], r2, e2, lim, cw, W["ln2_g"], W["ln2_b"])


_IN_SIZES = (512, 512, 512, 512, 4, 4, 256, 256, 512, 512, 16, 512, 768, 8, 3072)


def _prep_layer(w_in, m_i_bias, m_f_bias, m_norm, g_a_up, g_a_bias, g_norm,
                s_conv_w, s_conv_b, s_dt_bias, s_A_log, s_D, s_norm,
                w_branch, w_out, ln1_g, ln1_b, p_wq, p_keys, p_u, p_v, ln2_g, ln2_b):
    offs = [0]
    for s in _IN_SIZES:
        offs.append(offs[-1] + s)
    colw = lambda i: w_in[:, offs[i]:offs[i + 1]]
    (mq, mk, mv, mo, mi, mf, gq, gk, gv, gr, ga, sz, sxbc, sdt, gate) = [colw(i) for i in range(15)]
    pad = jnp.zeros((D_MODEL, LANE - (SM_DT + S_HEADS)), w_in.dtype)
    w1 = jnp.concatenate([mq, mk, mv, mo, gq, gk, gv, gr, sz, sxbc, mi, mf, ga, sdt, pad], axis=1)
    zl = lambda n: jnp.zeros((n,), f32)
    row = lambda a: a.astype(f32).reshape(1, -1)
    return dict(
        w1=w1.astype(bf16),
        w_gate=gate.astype(bf16),
        bias_s=row(jnp.concatenate([m_i_bias, m_f_bias, zl(G_RANK), s_dt_bias, zl(LANE - SM_DT - S_HEADS)])),
        alog=row(jnp.concatenate([zl(SM_DT), s_A_log, zl(LANE - SM_DT - S_HEADS)])),
        mnorm=row(m_norm), gnorm=row(g_norm), snorm=row(s_norm),
        drow=row(jnp.repeat(s_D, S_HD)),
        gup=jnp.zeros((LANE, G_KW), f32).at[SM_GA:SM_GA + G_RANK].set(g_a_up).astype(bf16),
        gab=row(g_a_bias),
        cw=s_conv_w.astype(f32), cb=row(s_conv_b),
        w_branch=w_branch.astype(bf16), w_out=w_out.astype(bf16),
        ln1_g=row(ln1_g), ln1_b=row(ln1_b),
        p_wq=p_wq.astype(bf16),
        p_keys=p_keys.astype(bf16).reshape(2 * P_HEADS, P_NKEYS, P_HALF),
        p_u=p_u.astype(bf16), p_vt=p_v.astype(bf16).T,
        ln2_g=row(ln2_g), ln2_b=row(ln2_b),
    )


def _run_layer(x, states, W, c, valid, tm, peer_tt):
    B, L, _ = x.shape
    nchunks = L // valid
    Lp = nchunks * c
    if c != valid:
        xp = jnp.pad(x.reshape(B, nchunks, valid, D_MODEL), ((0, 0), (0, 0), (0, c - valid), (0, 0)))
        xp = xp.reshape(B * Lp, D_MODEL)
    else:
        xp = x.reshape(B * L, D_MODEL)
    z = _matmul(xp, W["w1"], tm).reshape(B, Lp, Z_W)
    br, new_states = _mixer(z, states, W, c, valid)
    if c != valid:
        br = br.reshape(B, nchunks, c, N_BRANCH * BR_W)[:, :, :valid]
    xt = x.reshape(B * L, D_MODEL)
    h = _merge(xt, br.reshape(B * L, N_BRANCH * BR_W), W, tm)
    q16 = _wq_matmul(h, W["p_wq"], tm)
    route = _route(q16, W["p_keys"])
    y = _peer(h, route, W, peer_tt)
    return y.reshape(B, L, D_MODEL), new_states


def _zero_states(b):
    return (jnp.zeros((b, M_HEADS, M_HD, M_HD), f32), jnp.zeros((b, M_HEADS, M_HD), f32),
            jnp.zeros((b, M_HEADS), f32), jnp.zeros((b, G_HEADS, G_DK, G_DV), f32),
            jnp.zeros((b, S_HEADS, S_HD, S_STATE), f32), jnp.zeros((b, S_CONV - 1, S_XBC), f32))


PROMPT_CHUNK = 128
SAMPLE_CHUNK = 16


def kernel(x_prompt, x_sample, state_mlstm_C, state_mlstm_n, state_mlstm_m, state_gla_S, state_ssm_h, state_conv, w_in, m_i_bias, m_f_bias, m_norm, g_a_up, g_a_bias, g_norm, s_conv_w, s_conv_b, s_dt_bias, s_A_log, s_D, s_norm, w_branch, w_out, ln1_g, ln1_b, p_wq, p_keys, p_u, p_v, ln2_g, ln2_b):
    weights = (w_in, m_i_bias, m_f_bias, m_norm, g_a_up, g_a_bias, g_norm,
               s_conv_w, s_conv_b, s_dt_bias, s_A_log, s_D, s_norm,
               w_branch, w_out, ln1_g, ln1_b, p_wq, p_keys, p_u, p_v, ln2_g, ln2_b)
    in_states = (state_mlstm_C, state_mlstm_n, state_mlstm_m, state_gla_S, state_ssm_h, state_conv)
    hp, hs = x_prompt, x_sample
    ls = x_sample.shape[1]
    new_p = [[] for _ in range(6)]
    new_s = [[] for _ in range(6)]
    for l in range(DEPTH):
        W = _prep_layer(*[w[l] for w in weights])
        hp, sp = _run_layer(hp, _zero_states(hp.shape[0]), W, PROMPT_CHUNK, PROMPT_CHUNK, 256, 512)
        hs, ss = _run_layer(hs, tuple(s[l] for s in in_states), W, SAMPLE_CHUNK, ls, 256, 512)
        for j in range(6):
            new_p[j].append(sp[j])
            new_s[j].append(ss[j])
    P = [jnp.stack(a, axis=0) for a in new_p]
    S = [jnp.stack(a, axis=0) for a in new_s]
    return (hp, hs, P[0], P[1], P[2], P[3], P[4], P[5], S[0], S[1], S[2], S[3], S[4], S[5])
```

```python
import functools
import math

import numpy as np
import jax
import jax.numpy as jnp
from jax import lax
from jax.experimental import pallas as pl
from jax.experimental.pallas import tpu as pltpu

f32, bf16, i32 = jnp.float32, jnp.bfloat16, jnp.int32
NEG_INF = float("-inf")

D_MODEL = 1024
DEPTH = 2
BR_W = 512
N_BRANCH = 3
M_HEADS, M_HD = 4, 128
G_HEADS, G_DK, G_DV = 4, 64, 128
G_KW, G_VW = G_HEADS * G_DK, G_HEADS * G_DV
G_RANK = 16
G_TAU = 16.0
S_HD, S_HEADS, S_GROUPS, S_STATE, S_CONV = 64, 8, 2, 64, 4
S_HG = S_HEADS // S_GROUPS
S_W = S_HEADS * S_HD
S_XBC = S_W + 2 * S_GROUPS * S_STATE
P_HEADS, P_NKEYS, P_HALF, P_TOPK = 8, 128, 128, 16
P_EXPERTS = P_NKEYS * P_NKEYS
DN_ALPHA = (2.0 * DEPTH) ** 0.25
EPS = 1e-5

LANE = 128

Z_MQ, Z_MK, Z_MV, Z_MO = 0, 512, 1024, 1536
Z_GQ, Z_GK, Z_GV, Z_GR = 2048, 2304, 2560, 3072
Z_SZ, Z_XBC, Z_SMALL = 3584, 4096, 4864
Z_W = Z_SMALL + LANE
SM_MI, SM_MF, SM_GA, SM_DT = 0, 4, 8, 24

VMEM_LIMIT = 56 * 1024 * 1024

_NT = (((1,), (1,)), ((), ()))
_TN = (((0,), (0,)), ((), ()))
_HI = lax.Precision.HIGHEST


def _cparams(*sem):
    return pltpu.CompilerParams(dimension_semantics=sem, vmem_limit_bytes=VMEM_LIMIT)


def _mm_body(x_ref, w_ref, o_ref):
    o_ref[...] = jnp.dot(x_ref[...].astype(bf16), w_ref[...], preferred_element_type=f32)


def _matmul(x, w, tm):
    T, K = x.shape
    N = w.shape[1]
    return pl.pallas_call(
        _mm_body,
        grid=(T // tm,),
        in_specs=[pl.BlockSpec((tm, K), lambda i: (i, 0)), pl.BlockSpec((K, N), lambda i: (0, 0))],
        out_specs=pl.BlockSpec((tm, N), lambda i: (i, 0)),
        out_shape=jax.ShapeDtypeStruct((T, N), f32),
        compiler_params=_cparams("parallel"),
        name="in_proj",
    )(x, w)


def _wq_body(x_ref, w_ref, o_ref):
    r = jnp.dot(x_ref[...].astype(bf16), w_ref[...], preferred_element_type=f32)
    for g in range(2 * P_HEADS):
        o_ref[g] = r[:, g * P_HALF:(g + 1) * P_HALF]


def _wq_matmul(x, w, tm):
    T, K = x.shape
    G = 2 * P_HEADS
    return pl.pallas_call(
        _wq_body,
        grid=(T // tm,),
        in_specs=[pl.BlockSpec((tm, K), lambda i: (i, 0)), pl.BlockSpec((K, G * P_HALF), lambda i: (0, 0))],
        out_specs=pl.BlockSpec((G, tm, P_HALF), lambda i: (0, i, 0)),
        out_shape=jax.ShapeDtypeStruct((G, T, P_HALF), f32),
        compiler_params=_cparams("parallel"),
        name="peer_query",
    )(x, w)


def _mixer_body(c, valid, nb,
                z_ref, C0, n0, m0, S0, h0, buf0,
                bias_s, alog, mnorm, gnorm, snorm, drow, gup, gab, cw, cb,
                br_ref, C1, n1, m1, S1, h1, buf1,
                C_sc, n_sc, m_sc, S_sc, h_sc, xp_sc):
    j = pl.program_id(1)

    @pl.when(j == 0)
    def _load_state():
        C_sc[...] = C0[...]
        n_sc[...] = n0[...]
        for b in range(nb):
            for h in range(M_HEADS):
                m_sc[b, h:h + 1, :] = jnp.broadcast_to(m0[b, :, h:h + 1], (1, LANE))
        S_sc[...] = S0[...]
        h_sc[...] = h0[...]
        xp_sc[:, 5:8, :] = buf0[...]

    row = lax.broadcasted_iota(i32, (c, c), 0)
    col = lax.broadcasted_iota(i32, (c, c), 1)
    causal = row >= col
    tri = causal.astype(f32)
    lane = lax.broadcasted_iota(i32, (1, LANE), 1)
    tcol = lax.broadcasted_iota(i32, (c, 1), 0)
    tvalid = tcol < valid
    is_lf = (lane >> 2) == (SM_MF >> 2)
    is_dt = (lane >> 3) == (SM_DT >> 3)
    a_row = jnp.where(is_dt, -jnp.exp(alog[...]), 0.0)
    eye_f = (lax.broadcasted_iota(i32, (LANE, LANE), 0) == lax.broadcasted_iota(i32, (LANE, LANE), 1)).astype(f32)
    eye_b = eye_f.astype(bf16)
    tvalid_r = lax.broadcasted_iota(i32, (1, c), 1) < valid

    def xpose_b(x):
        n = x.shape[1]
        return lax.dot_general(eye_b[:n, :n], x, _NT, preferred_element_type=f32).astype(bf16)

    def xpose_f(x):
        return lax.dot_general(eye_f, x, _NT, precision=_HI, preferred_element_type=f32)


    def prologue(b):
        zs = z_ref[b, :, Z_SMALL:Z_SMALL + LANE]
        small = zs + bias_s[...]
        LI = jnp.where(tvalid, small, NEG_INF)
        LF = jnp.where(tvalid, jnp.where(is_lf, jax.nn.log_sigmoid(small), 0.0), 0.0)
        DT = jnp.where(tvalid, jnp.where(is_dt, jax.nn.softplus(small), 0.0), 0.0)
        cum = jnp.dot(tri, LF + DT * a_row, precision=_HI, preferred_element_type=f32)
        smallT = xpose_f(small)
        LIT = jnp.where(tvalid_r, smallT[SM_MI:SM_MI + F32_ROWS, :], NEG_INF)
        DTT = jnp.where(tvalid_r, jax.nn.softplus(smallT[SM_DT:SM_DT + S_HEADS, :]), 0.0)
        return dict(zs=zs, LI=LI, DT=DT, cum=cum, cumT=xpose_f(cum), LIT=LIT, DTT=DTT)

    def mlstm_head(b, h, P):
        sl = slice(h * M_HD, (h + 1) * M_HD)
        q = z_ref[b, :, Z_MQ + h * M_HD:Z_MQ + (h + 1) * M_HD] * (M_HD ** -0.5)
        k = z_ref[b, :, Z_MK + h * M_HD:Z_MK + (h + 1) * M_HD]
        v = z_ref[b, :, Z_MV + h * M_HD:Z_MV + (h + 1) * M_HD]
        qb, kb, vb = q.astype(bf16), k.astype(bf16), v.astype(bf16)
        Cm_ = C_sc[b, h]
        qk = lax.dot_general(qb, kb, _NT, preferred_element_type=f32)
        qC = lax.dot_general(qb, Cm_.astype(bf16), _NT, preferred_element_type=f32)
        yield
        b_c = P["cum"][:, SM_MF + h:SM_MF + h + 1]
        b_r = P["cumT"][SM_MF + h:SM_MF + h + 1, :]
        li_c = P["LI"][:, SM_MI + h:SM_MI + h + 1]
        li_r = P["LIT"][SM_MI + h:SM_MI + h + 1, :]
        m_prev = m_sc[b, h:h + 1, 0:1]
        a = b_c + m_prev
        d = jnp.where(causal, b_c - b_r + li_r, NEG_INF)
        m_t = jnp.maximum(a, jnp.max(d, axis=1, keepdims=True))
        s = qk * jnp.exp(d - m_t)
        e_in = jnp.exp(a - m_t)
        sv = jnp.dot(s.astype(bf16), vb, preferred_element_type=f32)
        b_last = b_c[c - 1:c, :]
        g_c = b_last - b_c + li_c
        m_new = jnp.maximum(b_last + m_prev, jnp.max(g_c, axis=0, keepdims=True))
        e_c = jnp.exp(b_last + m_prev - m_new)
        wg = jnp.exp(g_c - m_new)
        upd = jnp.dot(xpose_b((wg * v).astype(bf16)), kb, preferred_element_type=f32)
        yield
        num = sv + e_in * qC
        n_row = n_sc[b, h:h + 1, :]
        den = jnp.sum(s, axis=1, keepdims=True) + e_in * jnp.sum(q * n_row, axis=1, keepdims=True)
        hh = num / jnp.maximum(jnp.abs(den), jnp.exp(-m_t))
        C_sc[b, h] = e_c * Cm_ + upd
        n_sc[b, h:h + 1, :] = e_c * n_row + jnp.sum(wg * k, axis=0, keepdims=True)
        m_sc[b, h:h + 1, :] = jnp.broadcast_to(m_new, (1, LANE))
        mu = jnp.mean(hh, axis=1, keepdims=True)
        xc = hh - mu
        hn = xc * lax.rsqrt(jnp.mean(xc * xc, axis=1, keepdims=True) + EPS)
        mo = z_ref[b, :, Z_MO + h * M_HD:Z_MO + (h + 1) * M_HD]
        br_ref[b, :, sl] = hn * mnorm[:, sl] * jax.nn.sigmoid(mo)

    def gla_seq(b, P):
        ga = jnp.dot(P["zs"].astype(bf16), gup[...], preferred_element_type=f32) + gab[...]
        yield
        log_a = jnp.where(tvalid, jax.nn.log_sigmoid(ga) * (1.0 / G_TAU), 0.0)
        lam = jnp.dot(tri, log_a, precision=_HI, preferred_element_type=f32)
        yield
        levels = []
        m_half = c // 2
        while m_half >= 1:
            sh = int(math.log2(2 * m_half))
            sel = (col == ((row >> sh) << sh) + (m_half - 1)).astype(f32)
            levels.append((m_half, sh, jnp.dot(sel, lam, precision=_HI, preferred_element_type=f32)))
            m_half //= 2
        dec_c = [jnp.exp(xpose_f(lam[c - F32_ROWS:c, i * LANE:(i + 1) * LANE])[:, F32_ROWS - 1:F32_ROWS])
                 for i in range(G_KW // LANE)]
        gq = z_ref[b, :, Z_GQ:Z_GQ + G_KW] * (G_DK ** -0.5)
        gk = z_ref[b, :, Z_GK:Z_GK + G_KW]
        yield
        att = [None] * G_HEADS
        for m_half, sh, beta in levels:
            upper = (tcol & (2 * m_half - 1)) >= m_half
            qm = (gq * jnp.exp(jnp.where(upper, lam - beta, NEG_INF))).astype(bf16)
            km = (gk * jnp.exp(jnp.where(upper, NEG_INF, beta - lam))).astype(bf16)
            ps = [lax.dot_general(qm[:, h * G_DK:(h + 1) * G_DK], km[:, h * G_DK:(h + 1) * G_DK], _NT,
                                  preferred_element_type=f32) for h in range(G_HEADS)]
            yield
            if 2 * m_half < c:
                same_blk = (row >> sh) == (col >> sh)
                ps = [jnp.where(same_blk, p, 0.0) for p in ps]
            att = [p if a_ is None else a_ + p for a_, p in zip(att, ps)]
        q_in = (gq * jnp.exp(lam)).astype(bf16)
        k_out = (gk * jnp.exp(lam[c - 1:c, :] - lam)).astype(bf16)
        k_outT = [xpose_b(k_out[:, i * LANE:(i + 1) * LANE]) for i in range(G_KW // LANE)]
        hpl = LANE // G_DK
        for h in range(G_HEADS):
            ks = slice(h * G_DK, (h + 1) * G_DK)
            vs = slice(h * G_DV, (h + 1) * G_DV)
            v = z_ref[b, :, Z_GV + h * G_DV:Z_GV + (h + 1) * G_DV]
            vb = v.astype(bf16)
            S_ = S_sc[b, h]
            o_att = jnp.dot(att[h].astype(bf16), vb, preferred_element_type=f32)
            o_st = jnp.dot(q_in[:, ks], S_.astype(bf16), preferred_element_type=f32)
            rs = slice((h % hpl) * G_DK, (h % hpl + 1) * G_DK)
            upd = jnp.dot(k_outT[h // hpl][rs, :], vb, preferred_element_type=f32)
            yield
            diag = jnp.sum(gq[:, ks] * gk[:, ks], axis=1, keepdims=True)
            o = o_att + diag * v + o_st
            S_sc[b, h] = dec_c[h // hpl][rs, :] * S_ + upd
            on = o * lax.rsqrt(jnp.mean(o * o, axis=1, keepdims=True) + EPS)
            gr = z_ref[b, :, Z_GR + h * G_DV:Z_GR + (h + 1) * G_DV]
            br_ref[b, :, BR_W + h * G_DV:BR_W + (h + 1) * G_DV] = on * gnorm[:, vs] * jax.nn.silu(gr)

    def ssd_group(b, g, P, act):
        Bm = act[:, S_W + g * S_STATE:S_W + (g + 1) * S_STATE].astype(bf16)
        c0 = S_W + S_GROUPS * S_STATE + g * S_STATE
        Cmat = act[:, c0:c0 + S_STATE].astype(bf16)
        cbm = lax.dot_general(Cmat, Bm, _NT, preferred_element_type=f32)
        yield
        ys = []
        ssq = None
        for hg in range(S_HG):
            hh_ = g * S_HG + hg
            ps = slice(hh_ * S_HD, (hh_ + 1) * S_HD)
            xs = act[:, ps]
            dt_c = P["DT"][:, SM_DT + hh_:SM_DT + hh_ + 1]
            dt_r = P["DTT"][hh_:hh_ + 1, :]
            l_c = P["cum"][:, SM_DT + hh_:SM_DT + hh_ + 1]
            l_r = P["cumT"][SM_DT + hh_:SM_DT + hh_ + 1, :]
            decay = jnp.exp(jnp.where(causal, l_c - l_r, NEG_INF))
            w = (cbm * decay * dt_r).astype(bf16)
            hs_ = h_sc[b, hh_]
            l_last = l_c[c - 1:c, :]
            ws = jnp.exp(l_last - l_c) * dt_c
            y_in = jnp.dot(w, xs.astype(bf16), preferred_element_type=f32)
            y_st = lax.dot_general(Cmat, hs_.astype(bf16), _NT, preferred_element_type=f32)
            upd = jnp.dot(xpose_b((ws * xs).astype(bf16)), Bm, preferred_element_type=f32)
            yield
            y = y_in + y_st * jnp.exp(l_c)
            h_sc[b, hh_] = jnp.exp(l_last) * hs_ + upd
            yy = (y + drow[:, ps] * xs) * jax.nn.silu(z_ref[b, :, Z_SZ + hh_ * S_HD:Z_SZ + (hh_ + 1) * S_HD])
            ys.append(yy)
            sq = jnp.sum(yy * yy, axis=1, keepdims=True)
            ssq = sq if ssq is None else ssq + sq
        scale = lax.rsqrt(ssq * (1.0 / (S_W // S_GROUPS)) + EPS)
        for hg in range(S_HG):
            hh_ = g * S_HG + hg
            ps = slice(hh_ * S_HD, (hh_ + 1) * S_HD)
            br_ref[b, :, 2 * BR_W + hh_ * S_HD:2 * BR_W + (hh_ + 1) * S_HD] = ys[hg] * scale * snorm[:, ps]

    streams = []
    for b in range(nb):
        P = prologue(b)
        xp_sc[b, 8:8 + c, :] = z_ref[b, :, Z_XBC:Z_XBC + S_XBC]
        conv = cb[...] + cw[0:1, :] * xp_sc[b, pl.ds(5, c), :]
        for jj in range(1, S_CONV):
            conv = conv + cw[jj:jj + 1, :] * xp_sc[b, pl.ds(5 + jj, c), :]
        new_buf = xp_sc[b, pl.ds(5 + valid, S_CONV - 1), :]
        xp_sc[b, 5:8, :] = new_buf
        act = jax.nn.silu(conv)
        streams += [mlstm_head(b, h, P) for h in range(M_HEADS)]
        streams += [gla_seq(b, P)]
        streams += [ssd_group(b, g, P, act) for g in range(S_GROUPS)]
    while streams:
        alive = []
        for st in streams:
            try:
                next(st)
                alive.append(st)
            except StopIteration:
                pass
        streams = alive

    @pl.when(j == pl.num_programs(1) - 1)
    def _store_state():
        C1[...] = C_sc[...]
        n1[...] = n_sc[...]
        m1[...] = m_sc[...]
        S1[...] = S_sc[...]
        h1[...] = h_sc[...]
        buf1[...] = xp_sc[:, 5:8, :]


def _mixer(z, states, W, c, valid, nb):
    B, L, _ = z.shape
    nc = L // c
    C0, n0, m0, S0, h0, buf0 = states
    m0 = m0.reshape(B, 1, M_HEADS)

    def per_b(shape):
        nd = len(shape)
        return pl.BlockSpec((nb,) + shape, lambda b, j: (b,) + (0,) * nd)

    def const(shape):
        nd = len(shape)
        return pl.BlockSpec(shape, lambda b, j: (0,) * nd)

    small_w = (W["bias_s"], W["alog"], W["mnorm"], W["gnorm"], W["snorm"], W["drow"],
               W["gup"], W["gab"], W["cw"], W["cb"])
    st_shapes = [(M_HEADS, M_HD, M_HD), (M_HEADS, M_HD), (M_HEADS, LANE),
                 (G_HEADS, G_DK, G_DV), (S_HEADS, S_HD, S_STATE), (S_CONV - 1, S_XBC)]
    out = pl.pallas_call(
        functools.partial(_mixer_body, c, valid, nb),
        grid=(B // nb, nc),
        in_specs=[pl.BlockSpec((nb, c, Z_W), lambda b, j: (b, j, 0)),
                  per_b((M_HEADS, M_HD, M_HD)), per_b((M_HEADS, M_HD)), per_b((1, M_HEADS)),
                  per_b((G_HEADS, G_DK, G_DV)), per_b((S_HEADS, S_HD, S_STATE)), per_b((S_CONV - 1, S_XBC))]
                 + [const(w.shape) for w in small_w],
        out_specs=[pl.BlockSpec((nb, c, N_BRANCH * BR_W), lambda b, j: (b, j, 0))]
                  + [per_b(s) for s in st_shapes],
        out_shape=[jax.ShapeDtypeStruct((B, L, N_BRANCH * BR_W), f32)]
                  + [jax.ShapeDtypeStruct((B,) + s, f32) for s in st_shapes],
        scratch_shapes=[pltpu.VMEM((nb, M_HEADS, M_HD, M_HD), f32), pltpu.VMEM((nb, M_HEADS, M_HD), f32),
                        pltpu.VMEM((nb, M_HEADS, LANE), f32), pltpu.VMEM((nb, G_HEADS, G_DK, G_DV), f32),
                        pltpu.VMEM((nb, S_HEADS, S_HD, S_STATE), f32), pltpu.VMEM((nb, c + 8, S_XBC), f32)],
        compiler_params=_cparams("parallel", "arbitrary"),
        name="mixer",
    )(z, C0, n0, m0, S0, h0, buf0, *small_w)
    br, C1, n1, m1, S1, h1, buf1 = out
    return br, (C1, n1, m1[:, :, 0], S1, h1, buf1)


def _layernorm(x, g, b):
    mu = jnp.mean(x, axis=1, keepdims=True)
    xc = x - mu
    return xc * lax.rsqrt(jnp.mean(xc * xc, axis=1, keepdims=True) + EPS) * g + b


def _merge_body(x_ref, br_ref, wg_ref, wb_ref, wo_ref, g_ref, b_ref, o_ref):
    x = x_ref[...]
    xb = x.astype(bf16)
    mixed = None
    for n in range(N_BRANCH):
        gate = jnp.dot(xb, wg_ref[:, n * D_MODEL:(n + 1) * D_MODEL], preferred_element_type=f32)
        proj = jnp.dot(br_ref[:, n * BR_W:(n + 1) * BR_W].astype(bf16), wb_ref[n], preferred_element_type=f32)
        t = jax.nn.sigmoid(gate) * proj
        mixed = t if mixed is None else mixed + t
    y = jnp.dot(mixed.astype(bf16), wo_ref[...], preferred_element_type=f32)
    o_ref[...] = _layernorm(DN_ALPHA * x + y, g_ref[...], b_ref[...])


def _merge(x, br, W, tm):
    T = x.shape[0]
    full = lambda a: pl.BlockSpec(a.shape, lambda i: (0,) * a.ndim)
    ws = (W["w_gate"], W["w_branch"], W["w_out"], W["ln1_g"], W["ln1_b"])
    return pl.pallas_call(
        _merge_body,
        grid=(T // tm,),
        in_specs=[pl.BlockSpec((tm, D_MODEL), lambda i: (i, 0)),
                  pl.BlockSpec((tm, N_BRANCH * BR_W), lambda i: (i, 0))] + [full(w) for w in ws],
        out_specs=pl.BlockSpec((tm, D_MODEL), lambda i: (i, 0)),
        out_shape=jax.ShapeDtypeStruct((T, D_MODEL), f32),
        compiler_params=_cparams("parallel"),
        name="merge",
    )(x, br, *ws)


ROUTE_TT = 128
_STAIR = [(k1, k2) for k1 in range(P_TOPK) for k2 in range(P_TOPK // (k1 + 1))]
N_CAND = -(-len(_STAIR) // 8) * 8


def _cand_tables():
    p12 = np.zeros((N_CAND, 2 * P_TOPK), np.float32)
    pos = np.full((N_CAND, LANE), 4.0 * P_TOPK * P_TOPK, np.float32)
    bias = np.full((N_CAND, LANE), NEG_INF, np.float32)
    for r, (k1, k2) in enumerate(_STAIR):
        p12[r, k1] = 1.0
        p12[r, P_TOPK + k2] = 1.0
        pos[r, :] = k1 * P_TOPK + k2
        bias[r, :] = 0.0
    return jnp.asarray(p12), jnp.asarray(pos), jnp.asarray(bias)


def _route_body(q_ref, keys_ref, p12_ref, cpos_ref, cbias_ref, r2_ref, e2_ref, lim_ref, cw_ref, sc_sc, v_sc):
    tt = q_ref.shape[1]
    rowf = lax.broadcasted_iota(i32, (P_NKEYS, tt), 0).astype(f32)
    k1row = lax.broadcasted_iota(i32, (P_TOPK, 1), 0).astype(f32)
    cpos = cpos_ref[...]

    for g in range(2 * P_HEADS):
        sc_sc[g] = lax.dot_general(keys_ref[g], q_ref[g].astype(bf16), _NT, preferred_element_type=f32)

    A = B = C = None
    for step in range(P_HEADS + 2):
        if step < P_HEADS:
            A = dict(h=step, work=sc_sc[pl.ds(2 * step, 2)],
                     rank=jnp.full((2, P_NKEYS, tt), float(P_TOPK), f32))
        else:
            A = None
        for k in range(P_TOPK):
            if A is not None:
                work = A["work"]
                m = jnp.max(work, axis=1, keepdims=True)
                idx = jnp.min(jnp.where(work == m, rowf[None], float(P_NKEYS)), axis=1, keepdims=True)
                hit = rowf[None] == idx
                A["rank"] = jnp.where(hit, float(k), A["rank"])
                A["work"] = jnp.where(hit, NEG_INF, work)
                v_sc[A["h"] % 2, 0, k:k + 1, :] = m[0]
                v_sc[A["h"] % 2, 1, k:k + 1, :] = m[1]
            if B is not None:
                cand = B["cand"]
                m = jnp.max(cand, axis=0, keepdims=True)
                ps = jnp.min(jnp.where(cand == m, cpos, 4.0 * P_TOPK * P_TOPK), axis=0, keepdims=True)
                B["cand"] = jnp.where(cpos == ps, NEG_INF, cand)
                B["cnt"] = B["cnt"] + jnp.where(k1row == jnp.floor(ps * (1.0 / P_TOPK)), 1.0, 0.0)
                if k == 0:
                    B["top0"] = m
                    B["zsum"] = jnp.ones_like(m)
                else:
                    B["zsum"] = B["zsum"] + jnp.exp(m - B["top0"])
            if C is not None:
                C["lim"] = jnp.where(C["rank1"] == float(k), C["cnt"][k:k + 1, :], C["lim"])
        if C is not None:
            h = C["h"]
            lim_ref[h, 0] = C["lim"]
            cw_ref[h, 0] = jnp.exp(sc_sc[2 * h] - C["top1"]) / C["zsum"]
        C = None
        if B is not None:
            C = dict(h=B["h"], rank1=B["rank1"], cnt=B["cnt"], zsum=B["zsum"], top1=B["top1"],
                     lim=jnp.zeros((P_NKEYS, tt), f32))
        B = None
        if A is not None:
            h = A["h"]
            v1 = v_sc[h % 2, 0]
            v2 = v_sc[h % 2, 1]
            r2_ref[h] = A["rank"][1].astype(bf16)
            e2_ref[h] = jnp.exp(sc_sc[2 * h + 1] - v2[0:1, :]).astype(bf16)
            cand = (jnp.dot(p12_ref[:, 0:P_TOPK], v1, precision=_HI, preferred_element_type=f32)
                    + jnp.dot(p12_ref[:, P_TOPK:2 * P_TOPK], v2, precision=_HI, preferred_element_type=f32)
                    + cbias_ref[...])
            B = dict(h=h, cand=cand, cnt=jnp.zeros((P_TOPK, tt), f32), rank1=A["rank"][0], top1=v1[0:1, :])


def _route(q16, keys):
    G, T, _ = q16.shape
    tt = ROUTE_TT
    tabs = _cand_tables()
    spec = pl.BlockSpec((P_HEADS, P_NKEYS, tt), lambda i: (0, 0, i))
    full = lambda a: pl.BlockSpec(a.shape, lambda i: (0,) * a.ndim)
    return pl.pallas_call(
        _route_body,
        grid=(T // tt,),
        in_specs=[pl.BlockSpec((G, tt, P_HALF), lambda i: (0, i, 0)), full(keys)] + [full(t) for t in tabs],
        out_specs=[spec] * 2 + [pl.BlockSpec((P_HEADS, 1, P_NKEYS, tt), lambda i: (0, i, 0, 0))] * 2,
        out_shape=[jax.ShapeDtypeStruct((P_HEADS, P_NKEYS, T), bf16)] * 2
                  + [jax.ShapeDtypeStruct((P_HEADS, T // tt, P_NKEYS, tt), f32)] * 2,
        scratch_shapes=[pltpu.VMEM((2 * P_HEADS, P_NKEYS, tt), f32), pltpu.VMEM((2, 2, P_TOPK, tt), f32)],
        compiler_params=_cparams("parallel"),
        name="peer_route",
    )(q16, keys, *tabs)


PEER_NA = 4
PEER_EB = PEER_NA * P_NKEYS
BF16_ROWS = 16
F32_ROWS = 8


def _peer_weights(act, row0, r2_ref, e2_ref, lim_ref, cw_ref):
    tt = act.shape[1]
    grp = (P_NKEYS // BF16_ROWS, BF16_ROWS, tt)
    zero = jnp.zeros((), bf16)
    out = []
    for al in range(PEER_NA):
        gate = None
        for h in range(P_HEADS):
            r = row0 + al
            rep = lambda ref, j: jnp.broadcast_to(ref[h, j, r:r + 1, :], (F32_ROWS, LANE))
            lim = jnp.concatenate([rep(lim_ref, j) for j in range(tt // LANE)], axis=1)
            cw = jnp.concatenate([rep(cw_ref, j) for j in range(tt // LANE)], axis=1)
            lim = jnp.concatenate([lim, lim], axis=0).astype(bf16)
            cw = jnp.concatenate([cw, cw], axis=0).astype(bf16)
            t = jnp.where(r2_ref[h].reshape(grp) < lim[None], e2_ref[h].reshape(grp), zero) * cw[None]
            gate = t if gate is None else gate + t
        a = act[al * P_NKEYS:(al + 1) * P_NKEYS, :]
        gelu = (0.5 * a * (1.0 + lax.erf(a * (2.0 ** -0.5)))).astype(bf16)
        out.append((gelu.reshape(grp) * gate).reshape(P_NKEYS, tt))
    return jnp.concatenate(out, axis=0)


def _peer_body(x_ref, ua_ref, ub_ref, un_ref, vta_ref, vtb_ref, r2_ref, e2_ref, lim_ref, cw_ref,
               g_ref, b_ref, o_ref, xb_sc, acc_sc, act_sc):
    s = pl.program_id(1)

    @pl.when(s == 0)
    def _init():
        xb0 = x_ref[...].astype(bf16)
        xb_sc[...] = xb0
        acc_sc[...] = jnp.zeros_like(acc_sc)
        act_sc[...] = lax.dot_general(ua_ref[...], xb0, _NT, preferred_element_type=f32)

    xb = xb_sc[...]
    act_b = lax.dot_general(ub_ref[...], xb, _NT, preferred_element_type=f32)
    w_a = _peer_weights(act_sc[...], 0, r2_ref, e2_ref, lim_ref, cw_ref)
    acc_sc[...] += jnp.dot(vta_ref[...], w_a, preferred_element_type=f32)
    act_sc[...] = lax.dot_general(un_ref[...], xb, _NT, preferred_element_type=f32)
    w_b = _peer_weights(act_b, PEER_NA, r2_ref, e2_ref, lim_ref, cw_ref)
    acc_sc[...] += jnp.dot(vtb_ref[...], w_b, preferred_element_type=f32)

    @pl.when(s == pl.num_programs(1) - 1)
    def _finish():
        y = acc_sc[...].T
        o_ref[...] = _layernorm(DN_ALPHA * x_ref[...] + y, g_ref[...], b_ref[...])


def _peer(x, route, W, tt):
    T = x.shape[0]
    r2, e2, lim, cw = route
    nblk = P_EXPERTS // PEER_EB
    ns = nblk // 2
    tok = pl.BlockSpec((P_HEADS, P_NKEYS, tt), lambda i, s: (0, 0, i))
    per_a = pl.BlockSpec((P_HEADS, tt // LANE, 2 * PEER_NA, LANE), lambda i, s: (0, i, s, 0))
    vec = pl.BlockSpec((1, D_MODEL), lambda i, s: (0, 0))
    u_blk = lambda f: pl.BlockSpec((PEER_EB, D_MODEL), lambda i, s: (f(s), 0))
    vt_blk = lambda f: pl.BlockSpec((D_MODEL, PEER_EB), lambda i, s: (0, f(s)))
    return pl.pallas_call(
        _peer_body,
        grid=(T // tt, ns),
        in_specs=[pl.BlockSpec((tt, D_MODEL), lambda i, s: (i, 0)),
                  u_blk(lambda s: 2 * s), u_blk(lambda s: 2 * s + 1),
                  u_blk(lambda s: jnp.minimum(2 * s + 2, nblk - 1)),
                  vt_blk(lambda s: 2 * s), vt_blk(lambda s: 2 * s + 1),
                  tok, tok, per_a, per_a, vec, vec],
        out_specs=pl.BlockSpec((tt, D_MODEL), lambda i, s: (i, 0)),
        out_shape=jax.ShapeDtypeStruct((T, D_MODEL), f32),
        scratch_shapes=[pltpu.VMEM((tt, D_MODEL), bf16), pltpu.VMEM((D_MODEL, tt), f32),
                        pltpu.VMEM((PEER_EB, tt), f32)],
        compiler_params=_cparams("parallel", "arbitrary"),
        name="peer_experts",
    )(x, W["p_u"], W["p_u"], W["p_u"], W["p_vt"], W["p_vt"], r2, e2, lim, cw, W["ln2_g"], W["ln2_b"])


_IN_SIZES = (512, 512, 512, 512, 4, 4, 256, 256, 512, 512, 16, 512, 768, 8, 3072)


def _prep_layer(w_in, m_i_bias, m_f_bias, m_norm, g_a_up, g_a_bias, g_norm,
                s_conv_w, s_conv_b, s_dt_bias, s_A_log, s_D, s_norm,
                w_branch, w_out, ln1_g, ln1_b, p_wq, p_keys, p_u, p_v, ln2_g, ln2_b):
    offs = [0]
    for s in _IN_SIZES:
        offs.append(offs[-1] + s)
    colw = lambda i: w_in[:, offs[i]:offs[i + 1]]
    (mq, mk, mv, mo, mi, mf, gq, gk, gv, gr, ga, sz, sxbc, sdt, gate) = [colw(i) for i in range(15)]
    pad = jnp.zeros((D_MODEL, LANE - (SM_DT + S_HEADS)), w_in.dtype)
    w1 = jnp.concatenate([mq, mk, mv, mo, gq, gk, gv, gr, sz, sxbc, mi, mf, ga, sdt, pad], axis=1)
    zl = lambda n: jnp.zeros((n,), f32)
    row = lambda a: a.astype(f32).reshape(1, -1)
    return dict(
        w1=w1.astype(bf16),
        w_gate=gate.astype(bf16),
        bias_s=row(jnp.concatenate([m_i_bias, m_f_bias, zl(G_RANK), s_dt_bias, zl(LANE - SM_DT - S_HEADS)])),
        alog=row(jnp.concatenate([zl(SM_DT), s_A_log, zl(LANE - SM_DT - S_HEADS)])),
        mnorm=row(m_norm), gnorm=row(g_norm), snorm=row(s_norm),
        drow=row(jnp.repeat(s_D, S_HD)),
        gup=jnp.zeros((LANE, G_KW), f32).at[SM_GA:SM_GA + G_RANK].set(g_a_up).astype(bf16),
        gab=row(g_a_bias),
        cw=s_conv_w.astype(f32), cb=row(s_conv_b),
        w_branch=w_branch.astype(bf16), w_out=w_out.astype(bf16),
        ln1_g=row(ln1_g), ln1_b=row(ln1_b),
        p_wq=p_wq.astype(bf16),
        p_keys=p_keys.astype(bf16).reshape(2 * P_HEADS, P_NKEYS, P_HALF),
        p_u=p_u.astype(bf16), p_vt=p_v.astype(bf16).T,
        ln2_g=row(ln2_g), ln2_b=row(ln2_b),
    )


def _run_layer(x, states, W, c, valid, nb, tm, peer_tt):
    B, L, _ = x.shape
    nchunks = L // valid
    Lp = nchunks * c
    if c != valid:
        xp = jnp.pad(x.reshape(B, nchunks, valid, D_MODEL), ((0, 0), (0, 0), (0, c - valid), (0, 0)))
        xp = xp.reshape(B * Lp, D_MODEL)
    else:
        xp = x.reshape(B * L, D_MODEL)
    z = _matmul(xp, W["w1"], tm).reshape(B, Lp, Z_W)
    br, new_states = _mixer(z, states, W, c, valid, nb)
    if c != valid:
        br = br.reshape(B, nchunks, c, N_BRANCH * BR_W)[:, :, :valid]
    xt = x.reshape(B * L, D_MODEL)
    h = _merge(xt, br.reshape(B * L, N_BRANCH * BR_W), W, tm)
    q16 = _wq_matmul(h, W["p_wq"], tm)
    route = _route(q16, W["p_keys"])
    y = _peer(h, route, W, peer_tt)
    return y.reshape(B, L, D_MODEL), new_states


def _zero_states(b):
    return (jnp.zeros((b, M_HEADS, M_HD, M_HD), f32), jnp.zeros((b, M_HEADS, M_HD), f32),
            jnp.zeros((b, M_HEADS), f32), jnp.zeros((b, G_HEADS, G_DK, G_DV), f32),
            jnp.zeros((b, S_HEADS, S_HD, S_STATE), f32), jnp.zeros((b, S_CONV - 1, S_XBC), f32))


PROMPT_CHUNK = 128
SAMPLE_CHUNK = 16
PROMPT_NB = 2
SAMPLE_NB = 8


def kernel(x_prompt, x_sample, state_mlstm_C, state_mlstm_n, state_mlstm_m, state_gla_S, state_ssm_h, state_conv, w_in, m_i_bias, m_f_bias, m_norm, g_a_up, g_a_bias, g_norm, s_conv_w, s_conv_b, s_dt_bias, s_A_log, s_D, s_norm, w_branch, w_out, ln1_g, ln1_b, p_wq, p_keys, p_u, p_v, ln2_g, ln2_b):
    weights = (w_in, m_i_bias, m_f_bias, m_norm, g_a_up, g_a_bias, g_norm,
               s_conv_w, s_conv_b, s_dt_bias, s_A_log, s_D, s_norm,
               w_branch, w_out, ln1_g, ln1_b, p_wq, p_keys, p_u, p_v, ln2_g, ln2_b)
    in_states = (state_mlstm_C, state_mlstm_n, state_mlstm_m, state_gla_S, state_ssm_h, state_conv)
    hp, hs = x_prompt, x_sample
    ls = x_sample.shape[1]
    new_p = [[] for _ in range(6)]
    new_s = [[] for _ in range(6)]
    for l in range(DEPTH):
        W = _prep_layer(*[w[l] for w in weights])
        hp, sp = _run_layer(hp, _zero_states(hp.shape[0]), W, PROMPT_CHUNK, PROMPT_CHUNK, PROMPT_NB, 256, 512)
        hs, ss = _run_layer(hs, tuple(s[l] for s in in_states), W, SAMPLE_CHUNK, ls, SAMPLE_NB, 256, 512)
        for j in range(6):
            new_p[j].append(sp[j])
            new_s[j].append(ss[j])
    P = [jnp.stack(a, axis=0) for a in new_p]
    S = [jnp.stack(a, axis=0) for a in new_s]
    return (hp, hs, P[0], P[1], P[2], P[3], P[4], P[5], S[0], S[1], S[2], S[3], S[4], S[5])
```

```python
import functools
import math

import numpy as np
import jax
import jax.numpy as jnp
from jax import lax
from jax.experimental import pallas as pl
from jax.experimental.pallas import tpu as pltpu

f32, bf16, i32 = jnp.float32, jnp.bfloat16, jnp.int32
NEG_INF = float("-inf")

D_MODEL = 1024
DEPTH = 2
BR_W = 512
N_BRANCH = 3
M_HEADS, M_HD = 4, 128
G_HEADS, G_DK, G_DV = 4, 64, 128
G_KW, G_VW = G_HEADS * G_DK, G_HEADS * G_DV
G_RANK = 16
G_TAU = 16.0
S_HD, S_HEADS, S_GROUPS, S_STATE, S_CONV = 64, 8, 2, 64, 4
S_HG = S_HEADS // S_GROUPS
S_W = S_HEADS * S_HD
S_XBC = S_W + 2 * S_GROUPS * S_STATE
P_HEADS, P_NKEYS, P_HALF, P_TOPK = 8, 128, 128, 16
P_EXPERTS = P_NKEYS * P_NKEYS
DN_ALPHA = (2.0 * DEPTH) ** 0.25
EPS = 1e-5

LANE = 128

Z_MQ, Z_MK, Z_MV, Z_MO = 0, 512, 1024, 1536
Z_GQ, Z_GK, Z_GV, Z_GR = 2048, 2304, 2560, 3072
Z_SZ, Z_XBC, Z_SMALL = 3584, 4096, 4864
Z_W = Z_SMALL + LANE
SM_MI, SM_MF, SM_GA, SM_DT = 0, 4, 8, 24

VMEM_LIMIT = 56 * 1024 * 1024

_NT = (((1,), (1,)), ((), ()))
_TN = (((0,), (0,)), ((), ()))
_HI = lax.Precision.HIGHEST


def _cparams(*sem):
    return pltpu.CompilerParams(dimension_semantics=sem, vmem_limit_bytes=VMEM_LIMIT)


def _mm_body(x_ref, w_ref, o_ref):
    o_ref[...] = jnp.dot(x_ref[...].astype(bf16), w_ref[...], preferred_element_type=f32)


def _matmul(x, w, tm):
    T, K = x.shape
    N = w.shape[1]
    return pl.pallas_call(
        _mm_body,
        grid=(T // tm,),
        in_specs=[pl.BlockSpec((tm, K), lambda i: (i, 0)), pl.BlockSpec((K, N), lambda i: (0, 0))],
        out_specs=pl.BlockSpec((tm, N), lambda i: (i, 0)),
        out_shape=jax.ShapeDtypeStruct((T, N), f32),
        compiler_params=_cparams("parallel"),
        name="in_proj",
    )(x, w)


def _wq_body(x_ref, w_ref, o_ref):
    r = jnp.dot(x_ref[...].astype(bf16), w_ref[...], preferred_element_type=f32)
    for g in range(2 * P_HEADS):
        o_ref[g] = r[:, g * P_HALF:(g + 1) * P_HALF]


def _wq_matmul(x, w, tm):
    T, K = x.shape
    G = 2 * P_HEADS
    return pl.pallas_call(
        _wq_body,
        grid=(T // tm,),
        in_specs=[pl.BlockSpec((tm, K), lambda i: (i, 0)), pl.BlockSpec((K, G * P_HALF), lambda i: (0, 0))],
        out_specs=pl.BlockSpec((G, tm, P_HALF), lambda i: (0, i, 0)),
        out_shape=jax.ShapeDtypeStruct((G, T, P_HALF), f32),
        compiler_params=_cparams("parallel"),
        name="peer_query",
    )(x, w)


def _mixer_body(c, valid, nb,
                z_ref, C0, n0, m0, S0, h0, buf0,
                bias_s, alog, mnorm, gnorm, snorm, drow, gup, gab, cw, cb,
                br_ref, C1, n1, m1, S1, h1, buf1,
                C_sc, n_sc, m_sc, S_sc, h_sc, xp_sc):
    j = pl.program_id(1)

    @pl.when(j == 0)
    def _load_state():
        C_sc[...] = C0[...]
        n_sc[...] = n0[...]
        for b in range(nb):
            for h in range(M_HEADS):
                m_sc[b, h:h + 1, :] = jnp.broadcast_to(m0[b, :, h:h + 1], (1, LANE))
        S_sc[...] = S0[...]
        h_sc[...] = h0[...]
        xp_sc[:, 5:8, :] = buf0[...]

    row = lax.broadcasted_iota(i32, (c, c), 0)
    col = lax.broadcasted_iota(i32, (c, c), 1)
    causal = row >= col
    tri = causal.astype(f32)
    lane = lax.broadcasted_iota(i32, (1, LANE), 1)
    tcol = lax.broadcasted_iota(i32, (c, 1), 0)
    tvalid = tcol < valid
    is_lf = (lane >> 2) == (SM_MF >> 2)
    is_dt = (lane >> 3) == (SM_DT >> 3)
    a_row = jnp.where(is_dt, -jnp.exp(alog[...]), 0.0)
    eye_f = (lax.broadcasted_iota(i32, (LANE, LANE), 0) == lax.broadcasted_iota(i32, (LANE, LANE), 1)).astype(f32)
    eye_b = eye_f.astype(bf16)
    tvalid_r = lax.broadcasted_iota(i32, (1, c), 1) < valid

    def xpose_b(x):
        n = x.shape[1]
        return lax.dot_general(eye_b[:n, :n], x, _NT, preferred_element_type=f32).astype(bf16)

    def xpose_f(x):
        return lax.dot_general(eye_f, x, _NT, precision=_HI, preferred_element_type=f32)


    def prologue(b):
        zs = z_ref[b, :, Z_SMALL:Z_SMALL + LANE]
        small = zs + bias_s[...]
        LI = jnp.where(tvalid, small, NEG_INF)
        LF = jnp.where(tvalid, jnp.where(is_lf, jax.nn.log_sigmoid(small), 0.0), 0.0)
        DT = jnp.where(tvalid, jnp.where(is_dt, jax.nn.softplus(small), 0.0), 0.0)
        cum = jnp.dot(tri, LF + DT * a_row, precision=_HI, preferred_element_type=f32)
        smallT = xpose_f(small)
        LIT = jnp.where(tvalid_r, smallT[SM_MI:SM_MI + F32_ROWS, :], NEG_INF)
        DTT = jnp.where(tvalid_r, jax.nn.softplus(smallT[SM_DT:SM_DT + S_HEADS, :]), 0.0)
        return dict(zs=zs, LI=LI, DT=DT, cum=cum, cumT=xpose_f(cum), LIT=LIT, DTT=DTT)

    def mlstm_head(b, h, P):
        sl = slice(h * M_HD, (h + 1) * M_HD)
        q = z_ref[b, :, Z_MQ + h * M_HD:Z_MQ + (h + 1) * M_HD] * (M_HD ** -0.5)
        k = z_ref[b, :, Z_MK + h * M_HD:Z_MK + (h + 1) * M_HD]
        v = z_ref[b, :, Z_MV + h * M_HD:Z_MV + (h + 1) * M_HD]
        qb, kb, vb = q.astype(bf16), k.astype(bf16), v.astype(bf16)
        Cm_ = C_sc[b, h]
        qk = lax.dot_general(qb, kb, _NT, preferred_element_type=f32)
        qC = lax.dot_general(qb, Cm_.astype(bf16), _NT, preferred_element_type=f32)
        yield
        b_c = P["cum"][:, SM_MF + h:SM_MF + h + 1]
        b_r = P["cumT"][SM_MF + h:SM_MF + h + 1, :]
        li_c = P["LI"][:, SM_MI + h:SM_MI + h + 1]
        li_r = P["LIT"][SM_MI + h:SM_MI + h + 1, :]
        m_prev = m_sc[b, h:h + 1, 0:1]
        a = b_c + m_prev
        d = jnp.where(causal, b_c - b_r + li_r, NEG_INF)
        m_t = jnp.maximum(a, jnp.max(d, axis=1, keepdims=True))
        s = qk * jnp.exp(d - m_t)
        e_in = jnp.exp(a - m_t)
        sv = jnp.dot(s.astype(bf16), vb, preferred_element_type=f32)
        b_last = b_c[c - 1:c, :]
        g_c = b_last - b_c + li_c
        m_new = jnp.maximum(b_last + m_prev, jnp.max(g_c, axis=0, keepdims=True))
        e_c = jnp.exp(b_last + m_prev - m_new)
        wg = jnp.exp(g_c - m_new)
        upd = jnp.dot(xpose_b((wg * v).astype(bf16)), kb, preferred_element_type=f32)
        yield
        num = sv + e_in * qC
        n_row = n_sc[b, h:h + 1, :]
        den = jnp.sum(s, axis=1, keepdims=True) + e_in * jnp.sum(q * n_row, axis=1, keepdims=True)
        hh = num / jnp.maximum(jnp.abs(den), jnp.exp(-m_t))
        C_sc[b, h] = e_c * Cm_ + upd
        n_sc[b, h:h + 1, :] = e_c * n_row + jnp.sum(wg * k, axis=0, keepdims=True)
        m_sc[b, h:h + 1, :] = jnp.broadcast_to(m_new, (1, LANE))
        mu = jnp.mean(hh, axis=1, keepdims=True)
        xc = hh - mu
        hn = xc * lax.rsqrt(jnp.mean(xc * xc, axis=1, keepdims=True) + EPS)
        mo = z_ref[b, :, Z_MO + h * M_HD:Z_MO + (h + 1) * M_HD]
        br_ref[b, :, sl] = hn * mnorm[:, sl] * jax.nn.sigmoid(mo)

    def gla_seq(b, P):
        ga = jnp.dot(P["zs"].astype(bf16), gup[...], preferred_element_type=f32) + gab[...]
        yield
        log_a = jnp.where(tvalid, jax.nn.log_sigmoid(ga) * (1.0 / G_TAU), 0.0)
        lam = jnp.dot(tri, log_a, precision=_HI, preferred_element_type=f32)
        yield
        levels = []
        m_half = c // 2
        while m_half >= 1:
            sh = int(math.log2(2 * m_half))
            if 2 * m_half >= F32_ROWS:
                blk = lam.reshape(c // (2 * m_half), 2 * m_half, G_KW)[:, m_half - 1:m_half, :]
                beta = jnp.broadcast_to(blk, (c // (2 * m_half), 2 * m_half, G_KW)).reshape(c, G_KW)
            else:
                sel = (col == ((row >> sh) << sh) + (m_half - 1)).astype(f32)
                beta = jnp.dot(sel, lam, precision=_HI, preferred_element_type=f32)
            levels.append((m_half, sh, beta))
            m_half //= 2
        dec_c = [jnp.exp(xpose_f(lam[c - F32_ROWS:c, i * LANE:(i + 1) * LANE])[:, F32_ROWS - 1:F32_ROWS])
                 for i in range(G_KW // LANE)]
        gq = z_ref[b, :, Z_GQ:Z_GQ + G_KW] * (G_DK ** -0.5)
        gk = z_ref[b, :, Z_GK:Z_GK + G_KW]
        yield
        att = [None] * G_HEADS
        for m_half, sh, beta in levels:
            upper = (tcol & (2 * m_half - 1)) >= m_half
            qm = (gq * jnp.exp(jnp.where(upper, lam - beta, NEG_INF))).astype(bf16)
            km = (gk * jnp.exp(jnp.where(upper, NEG_INF, beta - lam))).astype(bf16)
            ps = [lax.dot_general(qm[:, h * G_DK:(h + 1) * G_DK], km[:, h * G_DK:(h + 1) * G_DK], _NT,
                                  preferred_element_type=f32) for h in range(G_HEADS)]
            yield
            if 2 * m_half < c:
                same_blk = (row >> sh) == (col >> sh)
                ps = [jnp.where(same_blk, p, 0.0) for p in ps]
            att = [p if a_ is None else a_ + p for a_, p in zip(att, ps)]
        q_in = (gq * jnp.exp(lam)).astype(bf16)
        k_out = (gk * jnp.exp(lam[c - 1:c, :] - lam)).astype(bf16)
        k_outT = [xpose_b(k_out[:, i * LANE:(i + 1) * LANE]) for i in range(G_KW // LANE)]
        hpl = LANE // G_DK
        for h in range(G_HEADS):
            ks = slice(h * G_DK, (h + 1) * G_DK)
            vs = slice(h * G_DV, (h + 1) * G_DV)
            v = z_ref[b, :, Z_GV + h * G_DV:Z_GV + (h + 1) * G_DV]
            vb = v.astype(bf16)
            S_ = S_sc[b, h]
            o_att = jnp.dot(att[h].astype(bf16), vb, preferred_element_type=f32)
            o_st = jnp.dot(q_in[:, ks], S_.astype(bf16), preferred_element_type=f32)
            rs = slice((h % hpl) * G_DK, (h % hpl + 1) * G_DK)
            upd = jnp.dot(k_outT[h // hpl][rs, :], vb, preferred_element_type=f32)
            yield
            diag = jnp.sum(gq[:, ks] * gk[:, ks], axis=1, keepdims=True)
            o = o_att + diag * v + o_st
            S_sc[b, h] = dec_c[h // hpl][rs, :] * S_ + upd
            on = o * lax.rsqrt(jnp.mean(o * o, axis=1, keepdims=True) + EPS)
            gr = z_ref[b, :, Z_GR + h * G_DV:Z_GR + (h + 1) * G_DV]
            br_ref[b, :, BR_W + h * G_DV:BR_W + (h + 1) * G_DV] = on * gnorm[:, vs] * jax.nn.silu(gr)

    def ssd_group(b, g, P, act):
        Bm = act[:, S_W + g * S_STATE:S_W + (g + 1) * S_STATE].astype(bf16)
        c0 = S_W + S_GROUPS * S_STATE + g * S_STATE
        Cmat = act[:, c0:c0 + S_STATE].astype(bf16)
        cbm = lax.dot_general(Cmat, Bm, _NT, preferred_element_type=f32)
        yield
        ys = []
        ssq = None
        for hg in range(S_HG):
            hh_ = g * S_HG + hg
            ps = slice(hh_ * S_HD, (hh_ + 1) * S_HD)
            xs = act[:, ps]
            dt_c = P["DT"][:, SM_DT + hh_:SM_DT + hh_ + 1]
            dt_r = P["DTT"][hh_:hh_ + 1, :]
            l_c = P["cum"][:, SM_DT + hh_:SM_DT + hh_ + 1]
            l_r = P["cumT"][SM_DT + hh_:SM_DT + hh_ + 1, :]
            decay = jnp.exp(jnp.where(causal, l_c - l_r, NEG_INF))
            w = (cbm * decay * dt_r).astype(bf16)
            hs_ = h_sc[b, hh_]
            l_last = l_c[c - 1:c, :]
            ws = jnp.exp(l_last - l_c) * dt_c
            y_in = jnp.dot(w, xs.astype(bf16), preferred_element_type=f32)
            y_st = lax.dot_general(Cmat, hs_.astype(bf16), _NT, preferred_element_type=f32)
            upd = jnp.dot(xpose_b((ws * xs).astype(bf16)), Bm, preferred_element_type=f32)
            yield
            y = y_in + y_st * jnp.exp(l_c)
            h_sc[b, hh_] = jnp.exp(l_last) * hs_ + upd
            yy = (y + drow[:, ps] * xs) * jax.nn.silu(z_ref[b, :, Z_SZ + hh_ * S_HD:Z_SZ + (hh_ + 1) * S_HD])
            ys.append(yy)
            sq = jnp.sum(yy * yy, axis=1, keepdims=True)
            ssq = sq if ssq is None else ssq + sq
        scale = lax.rsqrt(ssq * (1.0 / (S_W // S_GROUPS)) + EPS)
        for hg in range(S_HG):
            hh_ = g * S_HG + hg
            ps = slice(hh_ * S_HD, (hh_ + 1) * S_HD)
            br_ref[b, :, 2 * BR_W + hh_ * S_HD:2 * BR_W + (hh_ + 1) * S_HD] = ys[hg] * scale * snorm[:, ps]

    streams = []
    for b in range(nb):
        P = prologue(b)
        xp_sc[b, 8:8 + c, :] = z_ref[b, :, Z_XBC:Z_XBC + S_XBC]
        conv = cb[...] + cw[0:1, :] * xp_sc[b, pl.ds(5, c), :]
        for jj in range(1, S_CONV):
            conv = conv + cw[jj:jj + 1, :] * xp_sc[b, pl.ds(5 + jj, c), :]
        new_buf = xp_sc[b, pl.ds(5 + valid, S_CONV - 1), :]
        xp_sc[b, 5:8, :] = new_buf
        act = jax.nn.silu(conv)
        streams += [mlstm_head(b, h, P) for h in range(M_HEADS)]
        streams += [gla_seq(b, P)]
        streams += [ssd_group(b, g, P, act) for g in range(S_GROUPS)]
    while streams:
        alive = []
        for st in streams:
            try:
                next(st)
                alive.append(st)
            except StopIteration:
                pass
        streams = alive

    @pl.when(j == pl.num_programs(1) - 1)
    def _store_state():
        C1[...] = C_sc[...]
        n1[...] = n_sc[...]
        m1[...] = m_sc[...]
        S1[...] = S_sc[...]
        h1[...] = h_sc[...]
        buf1[...] = xp_sc[:, 5:8, :]


def _mixer(z, states, layer, W, c, valid, nb):
    B, L, _ = z.shape
    nc = L // c
    C0, n0, m0, S0, h0, buf0 = states
    m0 = m0.reshape(m0.shape[0], B, 1, M_HEADS)

    def per_b(shape):
        nd = len(shape)
        return pl.BlockSpec((nb,) + shape, lambda b, j: (b,) + (0,) * nd)

    def per_b_in(shape):
        nd = len(shape)
        return pl.BlockSpec((None, nb) + shape, lambda b, j: (layer, b) + (0,) * nd)

    def const(shape):
        nd = len(shape)
        return pl.BlockSpec(shape, lambda b, j: (0,) * nd)

    small_w = (W["bias_s"], W["alog"], W["mnorm"], W["gnorm"], W["snorm"], W["drow"],
               W["gup"], W["gab"], W["cw"], W["cb"])
    st_shapes = [(M_HEADS, M_HD, M_HD), (M_HEADS, M_HD), (M_HEADS, LANE),
                 (G_HEADS, G_DK, G_DV), (S_HEADS, S_HD, S_STATE), (S_CONV - 1, S_XBC)]
    out = pl.pallas_call(
        functools.partial(_mixer_body, c, valid, nb),
        grid=(B // nb, nc),
        in_specs=[pl.BlockSpec((nb, c, Z_W), lambda b, j: (b, j, 0)),
                  per_b_in((M_HEADS, M_HD, M_HD)), per_b_in((M_HEADS, M_HD)), per_b_in((1, M_HEADS)),
                  per_b_in((G_HEADS, G_DK, G_DV)), per_b_in((S_HEADS, S_HD, S_STATE)),
                  per_b_in((S_CONV - 1, S_XBC))]
                 + [const(w.shape) for w in small_w],
        out_specs=[pl.BlockSpec((nb, c, N_BRANCH * BR_W), lambda b, j: (b, j, 0))]
                  + [per_b(s) for s in st_shapes],
        out_shape=[jax.ShapeDtypeStruct((B, L, N_BRANCH * BR_W), f32)]
                  + [jax.ShapeDtypeStruct((B,) + s, f32) for s in st_shapes],
        scratch_shapes=[pltpu.VMEM((nb, M_HEADS, M_HD, M_HD), f32), pltpu.VMEM((nb, M_HEADS, M_HD), f32),
                        pltpu.VMEM((nb, M_HEADS, LANE), f32), pltpu.VMEM((nb, G_HEADS, G_DK, G_DV), f32),
                        pltpu.VMEM((nb, S_HEADS, S_HD, S_STATE), f32), pltpu.VMEM((nb, c + 8, S_XBC), f32)],
        compiler_params=_cparams("parallel", "arbitrary"),
        name="mixer",
    )(z, C0, n0, m0, S0, h0, buf0, *small_w)
    br, C1, n1, m1, S1, h1, buf1 = out
    return br, (C1, n1, m1[:, :, 0], S1, h1, buf1)


def _layernorm(x, g, b):
    mu = jnp.mean(x, axis=1, keepdims=True)
    xc = x - mu
    return xc * lax.rsqrt(jnp.mean(xc * xc, axis=1, keepdims=True) + EPS) * g + b


def _merge_body(x_ref, br_ref, wg_ref, wb_ref, wo_ref, g_ref, b_ref, o_ref):
    x = x_ref[...]
    xb = x.astype(bf16)
    mixed = None
    for n in range(N_BRANCH):
        gate = jnp.dot(xb, wg_ref[:, n * D_MODEL:(n + 1) * D_MODEL], preferred_element_type=f32)
        proj = jnp.dot(br_ref[:, n * BR_W:(n + 1) * BR_W].astype(bf16), wb_ref[n], preferred_element_type=f32)
        t = jax.nn.sigmoid(gate) * proj
        mixed = t if mixed is None else mixed + t
    y = jnp.dot(mixed.astype(bf16), wo_ref[...], preferred_element_type=f32)
    o_ref[...] = _layernorm(DN_ALPHA * x + y, g_ref[...], b_ref[...])


def _merge(x, br, W, tm):
    T = x.shape[0]
    full = lambda a: pl.BlockSpec(a.shape, lambda i: (0,) * a.ndim)
    ws = (W["w_gate"], W["w_branch"], W["w_out"], W["ln1_g"], W["ln1_b"])
    return pl.pallas_call(
        _merge_body,
        grid=(T // tm,),
        in_specs=[pl.BlockSpec((tm, D_MODEL), lambda i: (i, 0)),
                  pl.BlockSpec((tm, N_BRANCH * BR_W), lambda i: (i, 0))] + [full(w) for w in ws],
        out_specs=pl.BlockSpec((tm, D_MODEL), lambda i: (i, 0)),
        out_shape=jax.ShapeDtypeStruct((T, D_MODEL), f32),
        compiler_params=_cparams("parallel"),
        name="merge",
    )(x, br, *ws)


ROUTE_TT = 128
_STAIR = [(k1, k2) for k1 in range(P_TOPK) for k2 in range(P_TOPK // (k1 + 1))]
N_CAND = -(-len(_STAIR) // 8) * 8


def _cand_tables():
    p12 = np.zeros((N_CAND, 2 * P_TOPK), np.float32)
    pos = np.full((N_CAND, LANE), 4.0 * P_TOPK * P_TOPK, np.float32)
    bias = np.full((N_CAND, LANE), NEG_INF, np.float32)
    for r, (k1, k2) in enumerate(_STAIR):
        p12[r, k1] = 1.0
        p12[r, P_TOPK + k2] = 1.0
        pos[r, :] = k1 * P_TOPK + k2
        bias[r, :] = 0.0
    return jnp.asarray(p12), jnp.asarray(pos), jnp.asarray(bias)


def _route_body(q_ref, keys_ref, p12_ref, cpos_ref, cbias_ref, r2_ref, e2_ref, lim_ref, cw_ref, sc_sc, v_sc):
    tt = q_ref.shape[1]
    rowf = lax.broadcasted_iota(i32, (P_NKEYS, tt), 0).astype(f32)
    k1row = lax.broadcasted_iota(i32, (P_TOPK, 1), 0).astype(f32)
    cpos = cpos_ref[...]

    for g in range(2 * P_HEADS):
        sc_sc[g] = lax.dot_general(keys_ref[g], q_ref[g].astype(bf16), _NT, preferred_element_type=f32)

    A = B = C = None
    for step in range(P_HEADS + 2):
        if step < P_HEADS:
            A = dict(h=step, work=sc_sc[pl.ds(2 * step, 2)],
                     rank=jnp.full((2, P_NKEYS, tt), float(P_TOPK), f32))
        else:
            A = None
        for k in range(P_TOPK):
            if A is not None:
                work = A["work"]
                m = jnp.max(work, axis=1, keepdims=True)
                idx = jnp.min(jnp.where(work == m, rowf[None], float(P_NKEYS)), axis=1, keepdims=True)
                hit = rowf[None] == idx
                A["rank"] = jnp.where(hit, float(k), A["rank"])
                A["work"] = jnp.where(hit, NEG_INF, work)
                v_sc[A["h"] % 2, 0, k:k + 1, :] = m[0]
                v_sc[A["h"] % 2, 1, k:k + 1, :] = m[1]
            if B is not None:
                cand = B["cand"]
                m = jnp.max(cand, axis=0, keepdims=True)
                ps = jnp.min(jnp.where(cand == m, cpos, 4.0 * P_TOPK * P_TOPK), axis=0, keepdims=True)
                B["cand"] = jnp.where(cpos == ps, NEG_INF, cand)
                B["cnt"] = B["cnt"] + jnp.where(k1row == jnp.floor(ps * (1.0 / P_TOPK)), 1.0, 0.0)
                if k == 0:
                    B["top0"] = m
                    B["zsum"] = jnp.ones_like(m)
                else:
                    B["zsum"] = B["zsum"] + jnp.exp(m - B["top0"])
            if C is not None:
                C["lim"] = jnp.where(C["rank1"] == float(k), C["cnt"][k:k + 1, :], C["lim"])
        if C is not None:
            h = C["h"]
            lim_ref[h, 0] = C["lim"]
            cw_ref[h, 0] = jnp.exp(sc_sc[2 * h] - C["top1"]) / C["zsum"]
        C = None
        if B is not None:
            C = dict(h=B["h"], rank1=B["rank1"], cnt=B["cnt"], zsum=B["zsum"], top1=B["top1"],
                     lim=jnp.zeros((P_NKEYS, tt), f32))
        B = None
        if A is not None:
            h = A["h"]
            v1 = v_sc[h % 2, 0]
            v2 = v_sc[h % 2, 1]
            r2_ref[h] = A["rank"][1].astype(bf16)
            e2_ref[h] = jnp.exp(sc_sc[2 * h + 1] - v2[0:1, :]).astype(bf16)
            cand = (jnp.dot(p12_ref[:, 0:P_TOPK], v1, precision=_HI, preferred_element_type=f32)
                    + jnp.dot(p12_ref[:, P_TOPK:2 * P_TOPK], v2, precision=_HI, preferred_element_type=f32)
                    + cbias_ref[...])
            B = dict(h=h, cand=cand, cnt=jnp.zeros((P_TOPK, tt), f32), rank1=A["rank"][0], top1=v1[0:1, :])


def _route(q16, keys):
    G, T, _ = q16.shape
    tt = ROUTE_TT
    tabs = _cand_tables()
    spec = pl.BlockSpec((P_HEADS, P_NKEYS, tt), lambda i: (0, 0, i))
    full = lambda a: pl.BlockSpec(a.shape, lambda i: (0,) * a.ndim)
    return pl.pallas_call(
        _route_body,
        grid=(T // tt,),
        in_specs=[pl.BlockSpec((G, tt, P_HALF), lambda i: (0, i, 0)), full(keys)] + [full(t) for t in tabs],
        out_specs=[spec] * 2 + [pl.BlockSpec((P_HEADS, 1, P_NKEYS, tt), lambda i: (0, i, 0, 0))] * 2,
        out_shape=[jax.ShapeDtypeStruct((P_HEADS, P_NKEYS, T), bf16)] * 2
                  + [jax.ShapeDtypeStruct((P_HEADS, T // tt, P_NKEYS, tt), f32)] * 2,
        scratch_shapes=[pltpu.VMEM((2 * P_HEADS, P_NKEYS, tt), f32), pltpu.VMEM((2, 2, P_TOPK, tt), f32)],
        compiler_params=_cparams("parallel"),
        name="peer_route",
    )(q16, keys, *tabs)


PEER_NA = 4
PEER_EB = PEER_NA * P_NKEYS
BF16_ROWS = 16
F32_ROWS = 8


def _peer_weights(act, row0, r2_ref, e2_ref, lim_ref, cw_ref):
    tt = act.shape[1]
    grp = (P_NKEYS // BF16_ROWS, BF16_ROWS, tt)
    zero = jnp.zeros((), bf16)
    out = []
    for al in range(PEER_NA):
        gate = None
        for h in range(P_HEADS):
            r = row0 + al
            rep = lambda ref, j: jnp.broadcast_to(ref[h, j, r:r + 1, :], (F32_ROWS, LANE))
            lim = jnp.concatenate([rep(lim_ref, j) for j in range(tt // LANE)], axis=1)
            cw = jnp.concatenate([rep(cw_ref, j) for j in range(tt // LANE)], axis=1)
            lim = jnp.concatenate([lim, lim], axis=0).astype(bf16)
            cw = jnp.concatenate([cw, cw], axis=0).astype(bf16)
            t = jnp.where(r2_ref[h].reshape(grp) < lim[None], e2_ref[h].reshape(grp), zero) * cw[None]
            gate = t if gate is None else gate + t
        a = act[al * P_NKEYS:(al + 1) * P_NKEYS, :]
        gelu = (0.5 * a * (1.0 + lax.erf(a * (2.0 ** -0.5)))).astype(bf16)
        out.append((gelu.reshape(grp) * gate).reshape(P_NKEYS, tt))
    return jnp.concatenate(out, axis=0)


PEER_BPS = 4


def _peer_body(x_ref, u0_ref, u1_ref, u2_ref, u3_ref, un_ref, vt0_ref, vt1_ref, vt2_ref, vt3_ref,
               r2_ref, e2_ref, lim_ref, cw_ref, g_ref, b_ref, o_ref, xb_sc, acc_sc, actx_sc, acty_sc):
    s = pl.program_id(1)
    route = (r2_ref, e2_ref, lim_ref, cw_ref)

    @pl.when(s == 0)
    def _init():
        xb0 = x_ref[...].astype(bf16)
        xb_sc[...] = xb0
        acc_sc[...] = jnp.zeros_like(acc_sc)
        actx_sc[...] = lax.dot_general(u0_ref[...], xb0, _NT, preferred_element_type=f32)

    xb = xb_sc[...]
    first = lambda u_ref: lax.dot_general(u_ref[...], xb, _NT, preferred_element_type=f32)
    act1 = first(u1_ref)
    acty_sc[...] = first(u2_ref)
    w0 = _peer_weights(actx_sc[...], 0, *route)
    acc_sc[...] += jnp.dot(vt0_ref[...], w0, preferred_element_type=f32)
    w1 = _peer_weights(act1, PEER_NA, *route)
    acc_sc[...] += jnp.dot(vt1_ref[...], w1, preferred_element_type=f32)
    act3 = first(u3_ref)
    actx_sc[...] = first(un_ref)
    w2 = _peer_weights(acty_sc[...], 2 * PEER_NA, *route)
    acc_sc[...] += jnp.dot(vt2_ref[...], w2, preferred_element_type=f32)
    w3 = _peer_weights(act3, 3 * PEER_NA, *route)
    acc_sc[...] += jnp.dot(vt3_ref[...], w3, preferred_element_type=f32)

    @pl.when(s == pl.num_programs(1) - 1)
    def _finish():
        y = acc_sc[...].T
        o_ref[...] = _layernorm(DN_ALPHA * x_ref[...] + y, g_ref[...], b_ref[...])


def _peer(x, route, W, tt):
    T = x.shape[0]
    r2, e2, lim, cw = route
    nblk = P_EXPERTS // PEER_EB
    ns = nblk // PEER_BPS
    tok = pl.BlockSpec((P_HEADS, P_NKEYS, tt), lambda i, s: (0, 0, i))
    per_a = pl.BlockSpec((P_HEADS, tt // LANE, PEER_BPS * PEER_NA, LANE), lambda i, s: (0, i, s, 0))
    vec = pl.BlockSpec((1, D_MODEL), lambda i, s: (0, 0))
    u_blk = lambda k: pl.BlockSpec((PEER_EB, D_MODEL), lambda i, s: (jnp.minimum(PEER_BPS * s + k, nblk - 1), 0))
    vt_blk = lambda k: pl.BlockSpec((D_MODEL, PEER_EB), lambda i, s: (0, PEER_BPS * s + k))
    return pl.pallas_call(
        _peer_body,
        grid=(T // tt, ns),
        in_specs=[pl.BlockSpec((tt, D_MODEL), lambda i, s: (i, 0))]
                 + [u_blk(k) for k in range(PEER_BPS + 1)] + [vt_blk(k) for k in range(PEER_BPS)]
                 + [tok, tok, per_a, per_a, vec, vec],
        out_specs=pl.BlockSpec((tt, D_MODEL), lambda i, s: (i, 0)),
        out_shape=jax.ShapeDtypeStruct((T, D_MODEL), f32),
        scratch_shapes=[pltpu.VMEM((tt, D_MODEL), bf16), pltpu.VMEM((D_MODEL, tt), f32),
                        pltpu.VMEM((PEER_EB, tt), f32), pltpu.VMEM((PEER_EB, tt), f32)],
        compiler_params=_cparams("parallel", "arbitrary"),
        name="peer_experts",
    )(x, *([W["p_u"]] * (PEER_BPS + 1)), *([W["p_vt"]] * PEER_BPS), r2, e2, lim, cw, W["ln2_g"], W["ln2_b"])


_IN_SIZES = (512, 512, 512, 512, 4, 4, 256, 256, 512, 512, 16, 512, 768, 8, 3072)


def _prep_layer(w_in, m_i_bias, m_f_bias, m_norm, g_a_up, g_a_bias, g_norm,
                s_conv_w, s_conv_b, s_dt_bias, s_A_log, s_D, s_norm,
                w_branch, w_out, ln1_g, ln1_b, p_wq, p_keys, p_u, p_v, ln2_g, ln2_b):
    offs = [0]
    for s in _IN_SIZES:
        offs.append(offs[-1] + s)
    colw = lambda i: w_in[:, offs[i]:offs[i + 1]]
    (mq, mk, mv, mo, mi, mf, gq, gk, gv, gr, ga, sz, sxbc, sdt, gate) = [colw(i) for i in range(15)]
    pad = jnp.zeros((D_MODEL, LANE - (SM_DT + S_HEADS)), w_in.dtype)
    w1 = jnp.concatenate([mq, mk, mv, mo, gq, gk, gv, gr, sz, sxbc, mi, mf, ga, sdt, pad], axis=1)
    zl = lambda n: jnp.zeros((n,), f32)
    row = lambda a: a.astype(f32).reshape(1, -1)
    return dict(
        w1=w1.astype(bf16),
        w_gate=gate.astype(bf16),
        bias_s=row(jnp.concatenate([m_i_bias, m_f_bias, zl(G_RANK), s_dt_bias, zl(LANE - SM_DT - S_HEADS)])),
        alog=row(jnp.concatenate([zl(SM_DT), s_A_log, zl(LANE - SM_DT - S_HEADS)])),
        mnorm=row(m_norm), gnorm=row(g_norm), snorm=row(s_norm),
        drow=row(jnp.repeat(s_D, S_HD)),
        gup=jnp.zeros((LANE, G_KW), f32).at[SM_GA:SM_GA + G_RANK].set(g_a_up).astype(bf16),
        gab=row(g_a_bias),
        cw=s_conv_w.astype(f32), cb=row(s_conv_b),
        w_branch=w_branch.astype(bf16), w_out=w_out.astype(bf16),
        ln1_g=row(ln1_g), ln1_b=row(ln1_b),
        p_wq=p_wq.astype(bf16),
        p_keys=p_keys.astype(bf16).reshape(2 * P_HEADS, P_NKEYS, P_HALF),
        p_u=p_u.astype(bf16), p_vt=p_v.astype(bf16).T,
        ln2_g=row(ln2_g), ln2_b=row(ln2_b),
    )


def _run_layer(x, states, layer, W, c, valid, nb, tm, peer_tt):
    B, L, _ = x.shape
    nchunks = L // valid
    Lp = nchunks * c
    if c != valid:
        xp = jnp.pad(x.reshape(B, nchunks, valid, D_MODEL), ((0, 0), (0, 0), (0, c - valid), (0, 0)))
        xp = xp.reshape(B * Lp, D_MODEL)
    else:
        xp = x.reshape(B * L, D_MODEL)
    z = _matmul(xp, W["w1"], tm).reshape(B, Lp, Z_W)
    br, new_states = _mixer(z, states, layer, W, c, valid, nb)
    if c != valid:
        br = br.reshape(B, nchunks, c, N_BRANCH * BR_W)[:, :, :valid]
    xt = x.reshape(B * L, D_MODEL)
    h = _merge(xt, br.reshape(B * L, N_BRANCH * BR_W), W, tm)
    q16 = _wq_matmul(h, W["p_wq"], tm)
    route = _route(q16, W["p_keys"])
    y = _peer(h, route, W, peer_tt)
    return y.reshape(B, L, D_MODEL), new_states


def _zero_states(b):
    return (jnp.zeros((1, b, M_HEADS, M_HD, M_HD), f32), jnp.zeros((1, b, M_HEADS, M_HD), f32),
            jnp.zeros((1, b, M_HEADS), f32), jnp.zeros((1, b, G_HEADS, G_DK, G_DV), f32),
            jnp.zeros((1, b, S_HEADS, S_HD, S_STATE), f32), jnp.zeros((1, b, S_CONV - 1, S_XBC), f32))


PROMPT_CHUNK = 128
SAMPLE_CHUNK = 16
PROMPT_NB = 2
SAMPLE_NB = 8


def kernel(x_prompt, x_sample, state_mlstm_C, state_mlstm_n, state_mlstm_m, state_gla_S, state_ssm_h, state_conv, w_in, m_i_bias, m_f_bias, m_norm, g_a_up, g_a_bias, g_norm, s_conv_w, s_conv_b, s_dt_bias, s_A_log, s_D, s_norm, w_branch, w_out, ln1_g, ln1_b, p_wq, p_keys, p_u, p_v, ln2_g, ln2_b):
    weights = (w_in, m_i_bias, m_f_bias, m_norm, g_a_up, g_a_bias, g_norm,
               s_conv_w, s_conv_b, s_dt_bias, s_A_log, s_D, s_norm,
               w_branch, w_out, ln1_g, ln1_b, p_wq, p_keys, p_u, p_v, ln2_g, ln2_b)
    in_states = (state_mlstm_C, state_mlstm_n, state_mlstm_m, state_gla_S, state_ssm_h, state_conv)
    hp, hs = x_prompt, x_sample
    ls = x_sample.shape[1]
    new_p = [[] for _ in range(6)]
    new_s = [[] for _ in range(6)]
    for l in range(DEPTH):
        W = _prep_layer(*[w[l] for w in weights])
        hp, sp = _run_layer(hp, _zero_states(hp.shape[0]), 0, W, PROMPT_CHUNK, PROMPT_CHUNK, PROMPT_NB, 256, 512)
        hs, ss = _run_layer(hs, in_states, l, W, SAMPLE_CHUNK, ls, SAMPLE_NB, 256, 512)
        for j in range(6):
            new_p[j].append(sp[j])
            new_s[j].append(ss[j])
    P = [jnp.stack(a, axis=0) for a in new_p]
    S = [jnp.stack(a, axis=0) for a in new_s]
    return (hp, hs, P[0], P[1], P[2], P[3], P[4], P[5], S[0], S[1], S[2], S[3], S[4], S[5])
```

```python
import functools
import math

import numpy as np
import jax
import jax.numpy as jnp
from jax import lax
from jax.experimental import pallas as pl
from jax.experimental.pallas import tpu as pltpu

f32, bf16, i32 = jnp.float32, jnp.bfloat16, jnp.int32
NEG_INF = float("-inf")

D_MODEL = 1024
DEPTH = 2
BR_W = 512
N_BRANCH = 3
M_HEADS, M_HD = 4, 128
G_HEADS, G_DK, G_DV = 4, 64, 128
G_KW, G_VW = G_HEADS * G_DK, G_HEADS * G_DV
G_RANK = 16
G_TAU = 16.0
S_HD, S_HEADS, S_GROUPS, S_STATE, S_CONV = 64, 8, 2, 64, 4
S_HG = S_HEADS // S_GROUPS
S_W = S_HEADS * S_HD
S_XBC = S_W + 2 * S_GROUPS * S_STATE
P_HEADS, P_NKEYS, P_HALF, P_TOPK = 8, 128, 128, 16
P_EXPERTS = P_NKEYS * P_NKEYS
DN_ALPHA = (2.0 * DEPTH) ** 0.25
EPS = 1e-5

LANE = 128

Z_MQ, Z_MK, Z_MV, Z_MO = 0, 512, 1024, 1536
Z_GQ, Z_GK, Z_GV, Z_GR = 2048, 2304, 2560, 3072
Z_SZ, Z_XBC, Z_SMALL = 3584, 4096, 4864
Z_W = Z_SMALL + LANE
SM_MI, SM_MF, SM_GA, SM_DT = 0, 4, 8, 24

VMEM_LIMIT = 56 * 1024 * 1024

_NT = (((1,), (1,)), ((), ()))
_TN = (((0,), (0,)), ((), ()))
_HI = lax.Precision.HIGHEST


def _cparams(*sem):
    return pltpu.CompilerParams(dimension_semantics=sem, vmem_limit_bytes=VMEM_LIMIT)


def _mm_body(x_ref, w_ref, o_ref):
    o_ref[...] = jnp.dot(x_ref[...].astype(bf16), w_ref[...], preferred_element_type=f32)


def _matmul(x, w, layer, tm):
    T, K = x.shape
    N = w.shape[2]
    return pl.pallas_call(
        _mm_body,
        grid=(T // tm,),
        in_specs=[pl.BlockSpec((tm, K), lambda i: (i, 0)), pl.BlockSpec((None, K, N), lambda i: (layer, 0, 0))],
        out_specs=pl.BlockSpec((tm, N), lambda i: (i, 0)),
        out_shape=jax.ShapeDtypeStruct((T, N), f32),
        compiler_params=_cparams("parallel"),
        name="in_proj",
    )(x, w)


def _wq_body(x_ref, w_ref, o_ref):
    r = jnp.dot(x_ref[...].astype(bf16), w_ref[...], preferred_element_type=f32)
    for g in range(2 * P_HEADS):
        o_ref[g] = r[:, g * P_HALF:(g + 1) * P_HALF]


def _wq_matmul(x, w, layer, tm):
    T, K = x.shape
    G = 2 * P_HEADS
    return pl.pallas_call(
        _wq_body,
        grid=(T // tm,),
        in_specs=[pl.BlockSpec((tm, K), lambda i: (i, 0)),
                  pl.BlockSpec((None, K, G * P_HALF), lambda i: (layer, 0, 0))],
        out_specs=pl.BlockSpec((G, tm, P_HALF), lambda i: (0, i, 0)),
        out_shape=jax.ShapeDtypeStruct((G, T, P_HALF), f32),
        compiler_params=_cparams("parallel"),
        name="peer_query",
    )(x, w)


def _mixer_body(c, valid, nb,
                z_ref, C0, n0, m0, S0, h0, buf0,
                bias_s, alog, mnorm, gnorm, snorm, drow, gup, gab, cw, cb,
                br_ref, C1, n1, m1, S1, h1, buf1,
                C_sc, n_sc, m_sc, S_sc, h_sc, xp_sc):
    j = pl.program_id(1)

    @pl.when(j == 0)
    def _load_state():
        C_sc[...] = C0[...]
        n_sc[...] = n0[...]
        for b in range(nb):
            for h in range(M_HEADS):
                m_sc[b, h:h + 1, :] = jnp.broadcast_to(m0[b, :, h:h + 1], (1, LANE))
        S_sc[...] = S0[...]
        h_sc[...] = h0[...]
        xp_sc[:, 5:8, :] = buf0[...]

    row = lax.broadcasted_iota(i32, (c, c), 0)
    col = lax.broadcasted_iota(i32, (c, c), 1)
    causal = row >= col
    tri = causal.astype(f32)
    lane = lax.broadcasted_iota(i32, (1, LANE), 1)
    tcol = lax.broadcasted_iota(i32, (c, 1), 0)
    tvalid = tcol < valid
    is_lf = (lane >> 2) == (SM_MF >> 2)
    is_dt = (lane >> 3) == (SM_DT >> 3)
    a_row = jnp.where(is_dt, -jnp.exp(alog[...]), 0.0)
    eye_f = (lax.broadcasted_iota(i32, (LANE, LANE), 0) == lax.broadcasted_iota(i32, (LANE, LANE), 1)).astype(f32)
    eye_b = eye_f.astype(bf16)
    tvalid_r = lax.broadcasted_iota(i32, (1, c), 1) < valid

    def xpose_b(x):
        n = x.shape[1]
        return lax.dot_general(eye_b[:n, :n], x, _NT, preferred_element_type=f32).astype(bf16)

    def xpose_f(x):
        return lax.dot_general(eye_f, x, _NT, precision=_HI, preferred_element_type=f32)


    def prologue(b):
        zs = z_ref[b, :, Z_SMALL:Z_SMALL + LANE]
        small = zs + bias_s[...]
        LI = jnp.where(tvalid, small, NEG_INF)
        LF = jnp.where(tvalid, jnp.where(is_lf, jax.nn.log_sigmoid(small), 0.0), 0.0)
        DT = jnp.where(tvalid, jnp.where(is_dt, jax.nn.softplus(small), 0.0), 0.0)
        cum = jnp.dot(tri, LF + DT * a_row, precision=_HI, preferred_element_type=f32)
        smallT = xpose_f(small)
        LIT = jnp.where(tvalid_r, smallT[SM_MI:SM_MI + F32_ROWS, :], NEG_INF)
        DTT = jnp.where(tvalid_r, jax.nn.softplus(smallT[SM_DT:SM_DT + S_HEADS, :]), 0.0)
        return dict(zs=zs, LI=LI, DT=DT, cum=cum, cumT=xpose_f(cum), LIT=LIT, DTT=DTT)

    def mlstm_head(b, h, P):
        sl = slice(h * M_HD, (h + 1) * M_HD)
        q = z_ref[b, :, Z_MQ + h * M_HD:Z_MQ + (h + 1) * M_HD] * (M_HD ** -0.5)
        k = z_ref[b, :, Z_MK + h * M_HD:Z_MK + (h + 1) * M_HD]
        v = z_ref[b, :, Z_MV + h * M_HD:Z_MV + (h + 1) * M_HD]
        qb, kb, vb = q.astype(bf16), k.astype(bf16), v.astype(bf16)
        Cm_ = C_sc[b, h]
        qk = lax.dot_general(qb, kb, _NT, preferred_element_type=f32)
        qC = lax.dot_general(qb, Cm_.astype(bf16), _NT, preferred_element_type=f32)
        yield
        b_c = P["cum"][:, SM_MF + h:SM_MF + h + 1]
        b_r = P["cumT"][SM_MF + h:SM_MF + h + 1, :]
        li_c = P["LI"][:, SM_MI + h:SM_MI + h + 1]
        li_r = P["LIT"][SM_MI + h:SM_MI + h + 1, :]
        m_prev = m_sc[b, h:h + 1, 0:1]
        a = b_c + m_prev
        d = jnp.where(causal, b_c - b_r + li_r, NEG_INF)
        m_t = jnp.maximum(a, jnp.max(d, axis=1, keepdims=True))
        s = qk * jnp.exp(d - m_t)
        e_in = jnp.exp(a - m_t)
        sv = jnp.dot(s.astype(bf16), vb, preferred_element_type=f32)
        b_last = b_c[c - 1:c, :]
        g_c = b_last - b_c + li_c
        m_new = jnp.maximum(b_last + m_prev, jnp.max(g_c, axis=0, keepdims=True))
        e_c = jnp.exp(b_last + m_prev - m_new)
        wg = jnp.exp(g_c - m_new)
        upd = jnp.dot(xpose_b((wg * v).astype(bf16)), kb, preferred_element_type=f32)
        yield
        num = sv + e_in * qC
        n_row = n_sc[b, h:h + 1, :]
        den = jnp.sum(s, axis=1, keepdims=True) + e_in * jnp.sum(q * n_row, axis=1, keepdims=True)
        hh = num / jnp.maximum(jnp.abs(den), jnp.exp(-m_t))
        C_sc[b, h] = e_c * Cm_ + upd
        n_sc[b, h:h + 1, :] = e_c * n_row + jnp.sum(wg * k, axis=0, keepdims=True)
        m_sc[b, h:h + 1, :] = jnp.broadcast_to(m_new, (1, LANE))
        mu = jnp.mean(hh, axis=1, keepdims=True)
        xc = hh - mu
        hn = xc * lax.rsqrt(jnp.mean(xc * xc, axis=1, keepdims=True) + EPS)
        mo = z_ref[b, :, Z_MO + h * M_HD:Z_MO + (h + 1) * M_HD]
        br_ref[b, :, sl] = hn * mnorm[:, sl] * jax.nn.sigmoid(mo)

    def gla_seq(b, P):
        ga = jnp.dot(P["zs"].astype(bf16), gup[...], preferred_element_type=f32) + gab[...]
        yield
        log_a = jnp.where(tvalid, jax.nn.log_sigmoid(ga) * (1.0 / G_TAU), 0.0)
        lam = jnp.dot(tri, log_a, precision=_HI, preferred_element_type=f32)
        yield
        levels = []
        m_half = c // 2
        while m_half >= 1:
            sh = int(math.log2(2 * m_half))
            if 2 * m_half >= F32_ROWS:
                blk = lam.reshape(c // (2 * m_half), 2 * m_half, G_KW)[:, m_half - 1:m_half, :]
                beta = jnp.broadcast_to(blk, (c // (2 * m_half), 2 * m_half, G_KW)).reshape(c, G_KW)
            else:
                sel = (col == ((row >> sh) << sh) + (m_half - 1)).astype(f32)
                beta = jnp.dot(sel, lam, precision=_HI, preferred_element_type=f32)
            levels.append((m_half, sh, beta))
            m_half //= 2
        dec_c = [jnp.exp(xpose_f(lam[c - F32_ROWS:c, i * LANE:(i + 1) * LANE])[:, F32_ROWS - 1:F32_ROWS])
                 for i in range(G_KW // LANE)]
        gq = z_ref[b, :, Z_GQ:Z_GQ + G_KW] * (G_DK ** -0.5)
        gk = z_ref[b, :, Z_GK:Z_GK + G_KW]
        yield
        att = [None] * G_HEADS
        for m_half, sh, beta in levels:
            upper = (tcol & (2 * m_half - 1)) >= m_half
            qm = (gq * jnp.exp(jnp.where(upper, lam - beta, NEG_INF))).astype(bf16)
            km = (gk * jnp.exp(jnp.where(upper, NEG_INF, beta - lam))).astype(bf16)
            ps = [lax.dot_general(qm[:, h * G_DK:(h + 1) * G_DK], km[:, h * G_DK:(h + 1) * G_DK], _NT,
                                  preferred_element_type=f32) for h in range(G_HEADS)]
            yield
            if 2 * m_half < c:
                same_blk = (row >> sh) == (col >> sh)
                ps = [jnp.where(same_blk, p, 0.0) for p in ps]
            att = [p if a_ is None else a_ + p for a_, p in zip(att, ps)]
        q_in = (gq * jnp.exp(lam)).astype(bf16)
        k_out = (gk * jnp.exp(lam[c - 1:c, :] - lam)).astype(bf16)
        k_outT = [xpose_b(k_out[:, i * LANE:(i + 1) * LANE]) for i in range(G_KW // LANE)]
        hpl = LANE // G_DK
        for h in range(G_HEADS):
            ks = slice(h * G_DK, (h + 1) * G_DK)
            vs = slice(h * G_DV, (h + 1) * G_DV)
            v = z_ref[b, :, Z_GV + h * G_DV:Z_GV + (h + 1) * G_DV]
            vb = v.astype(bf16)
            S_ = S_sc[b, h]
            o_att = jnp.dot(att[h].astype(bf16), vb, preferred_element_type=f32)
            o_st = jnp.dot(q_in[:, ks], S_.astype(bf16), preferred_element_type=f32)
            rs = slice((h % hpl) * G_DK, (h % hpl + 1) * G_DK)
            upd = jnp.dot(k_outT[h // hpl][rs, :], vb, preferred_element_type=f32)
            yield
            diag = jnp.sum(gq[:, ks] * gk[:, ks], axis=1, keepdims=True)
            o = o_att + diag * v + o_st
            S_sc[b, h] = dec_c[h // hpl][rs, :] * S_ + upd
            on = o * lax.rsqrt(jnp.mean(o * o, axis=1, keepdims=True) + EPS)
            gr = z_ref[b, :, Z_GR + h * G_DV:Z_GR + (h + 1) * G_DV]
            br_ref[b, :, BR_W + h * G_DV:BR_W + (h + 1) * G_DV] = on * gnorm[:, vs] * jax.nn.silu(gr)

    def ssd_group(b, g, P, act):
        Bm = act[:, S_W + g * S_STATE:S_W + (g + 1) * S_STATE].astype(bf16)
        c0 = S_W + S_GROUPS * S_STATE + g * S_STATE
        Cmat = act[:, c0:c0 + S_STATE].astype(bf16)
        cbm = lax.dot_general(Cmat, Bm, _NT, preferred_element_type=f32)
        yield
        ys = []
        ssq = None
        for hg in range(S_HG):
            hh_ = g * S_HG + hg
            ps = slice(hh_ * S_HD, (hh_ + 1) * S_HD)
            xs = act[:, ps]
            dt_c = P["DT"][:, SM_DT + hh_:SM_DT + hh_ + 1]
            dt_r = P["DTT"][hh_:hh_ + 1, :]
            l_c = P["cum"][:, SM_DT + hh_:SM_DT + hh_ + 1]
            l_r = P["cumT"][SM_DT + hh_:SM_DT + hh_ + 1, :]
            decay = jnp.exp(jnp.where(causal, l_c - l_r, NEG_INF))
            w = (cbm * decay * dt_r).astype(bf16)
            hs_ = h_sc[b, hh_]
            l_last = l_c[c - 1:c, :]
            ws = jnp.exp(l_last - l_c) * dt_c
            y_in = jnp.dot(w, xs.astype(bf16), preferred_element_type=f32)
            y_st = lax.dot_general(Cmat, hs_.astype(bf16), _NT, preferred_element_type=f32)
            upd = jnp.dot(xpose_b((ws * xs).astype(bf16)), Bm, preferred_element_type=f32)
            yield
            y = y_in + y_st * jnp.exp(l_c)
            h_sc[b, hh_] = jnp.exp(l_last) * hs_ + upd
            yy = (y + drow[:, ps] * xs) * jax.nn.silu(z_ref[b, :, Z_SZ + hh_ * S_HD:Z_SZ + (hh_ + 1) * S_HD])
            ys.append(yy)
            sq = jnp.sum(yy * yy, axis=1, keepdims=True)
            ssq = sq if ssq is None else ssq + sq
        scale = lax.rsqrt(ssq * (1.0 / (S_W // S_GROUPS)) + EPS)
        for hg in range(S_HG):
            hh_ = g * S_HG + hg
            ps = slice(hh_ * S_HD, (hh_ + 1) * S_HD)
            br_ref[b, :, 2 * BR_W + hh_ * S_HD:2 * BR_W + (hh_ + 1) * S_HD] = ys[hg] * scale * snorm[:, ps]

    streams = []
    for b in range(nb):
        P = prologue(b)
        xp_sc[b, 8:8 + c, :] = z_ref[b, :, Z_XBC:Z_XBC + S_XBC]
        conv = cb[...] + cw[0:1, :] * xp_sc[b, pl.ds(5, c), :]
        for jj in range(1, S_CONV):
            conv = conv + cw[jj:jj + 1, :] * xp_sc[b, pl.ds(5 + jj, c), :]
        new_buf = xp_sc[b, pl.ds(5 + valid, S_CONV - 1), :]
        xp_sc[b, 5:8, :] = new_buf
        act = jax.nn.silu(conv)
        streams += [mlstm_head(b, h, P) for h in range(M_HEADS)]
        streams += [gla_seq(b, P)]
        streams += [ssd_group(b, g, P, act) for g in range(S_GROUPS)]
    while streams:
        alive = []
        for st in streams:
            try:
                next(st)
                alive.append(st)
            except StopIteration:
                pass
        streams = alive

    @pl.when(j == pl.num_programs(1) - 1)
    def _store_state():
        C1[...] = C_sc[...]
        n1[...] = n_sc[...]
        m1[...] = m_sc[...]
        S1[...] = S_sc[...]
        h1[...] = h_sc[...]
        buf1[...] = xp_sc[:, 5:8, :]


def _mixer(z, states, layer, W, c, valid, nb):
    B, L, _ = z.shape
    nc = L // c
    C0, n0, m0, S0, h0, buf0 = states
    m0 = m0.reshape(m0.shape[0], B, 1, M_HEADS)

    def per_b(shape):
        nd = len(shape)
        return pl.BlockSpec((nb,) + shape, lambda b, j: (b,) + (0,) * nd)

    def per_b_in(shape):
        nd = len(shape)
        return pl.BlockSpec((None, nb) + shape, lambda b, j: (layer, b) + (0,) * nd)

    def const(shape):
        nd = len(shape)
        return pl.BlockSpec(shape, lambda b, j: (0,) * nd)

    small_w = (W["bias_s"], W["alog"], W["mnorm"], W["gnorm"], W["snorm"], W["drow"],
               W["gup"], W["gab"], W["cw"], W["cb"])
    st_shapes = [(M_HEADS, M_HD, M_HD), (M_HEADS, M_HD), (M_HEADS, LANE),
                 (G_HEADS, G_DK, G_DV), (S_HEADS, S_HD, S_STATE), (S_CONV - 1, S_XBC)]
    out = pl.pallas_call(
        functools.partial(_mixer_body, c, valid, nb),
        grid=(B // nb, nc),
        in_specs=[pl.BlockSpec((nb, c, Z_W), lambda b, j: (b, j, 0)),
                  per_b_in((M_HEADS, M_HD, M_HD)), per_b_in((M_HEADS, M_HD)), per_b_in((1, M_HEADS)),
                  per_b_in((G_HEADS, G_DK, G_DV)), per_b_in((S_HEADS, S_HD, S_STATE)),
                  per_b_in((S_CONV - 1, S_XBC))]
                 + [const(w.shape) for w in small_w],
        out_specs=[pl.BlockSpec((nb, c, N_BRANCH * BR_W), lambda b, j: (b, j, 0))]
                  + [per_b(s) for s in st_shapes],
        out_shape=[jax.ShapeDtypeStruct((B, L, N_BRANCH * BR_W), f32)]
                  + [jax.ShapeDtypeStruct((B,) + s, f32) for s in st_shapes],
        scratch_shapes=[pltpu.VMEM((nb, M_HEADS, M_HD, M_HD), f32), pltpu.VMEM((nb, M_HEADS, M_HD), f32),
                        pltpu.VMEM((nb, M_HEADS, LANE), f32), pltpu.VMEM((nb, G_HEADS, G_DK, G_DV), f32),
                        pltpu.VMEM((nb, S_HEADS, S_HD, S_STATE), f32), pltpu.VMEM((nb, c + 8, S_XBC), f32)],
        compiler_params=_cparams("parallel", "arbitrary"),
        name="mixer",
    )(z, C0, n0, m0, S0, h0, buf0, *small_w)
    br, C1, n1, m1, S1, h1, buf1 = out
    return br, (C1, n1, m1[:, :, 0], S1, h1, buf1)


def _layernorm(x, g, b):
    mu = jnp.mean(x, axis=1, keepdims=True)
    xc = x - mu
    return xc * lax.rsqrt(jnp.mean(xc * xc, axis=1, keepdims=True) + EPS) * g + b


def _merge_body(x_ref, br_ref, wg_ref, wb_ref, wo_ref, g_ref, b_ref, o_ref):
    x = x_ref[...]
    xb = x.astype(bf16)
    mixed = None
    for n in range(N_BRANCH):
        gate = jnp.dot(xb, wg_ref[:, n * D_MODEL:(n + 1) * D_MODEL], preferred_element_type=f32)
        proj = jnp.dot(br_ref[:, n * BR_W:(n + 1) * BR_W].astype(bf16), wb_ref[n], preferred_element_type=f32)
        t = jax.nn.sigmoid(gate) * proj
        mixed = t if mixed is None else mixed + t
    y = jnp.dot(mixed.astype(bf16), wo_ref[...], preferred_element_type=f32)
    o_ref[...] = _layernorm(DN_ALPHA * x + y, g_ref[...], b_ref[...])


def _merge(x, br, W, layer, tm):
    T = x.shape[0]
    full = lambda a: pl.BlockSpec((None,) + a.shape[1:], lambda i: (layer,) + (0,) * (a.ndim - 1))
    ws = (W["w_gate"], W["w_branch"], W["w_out"], W["ln1_g"], W["ln1_b"])
    return pl.pallas_call(
        _merge_body,
        grid=(T // tm,),
        in_specs=[pl.BlockSpec((tm, D_MODEL), lambda i: (i, 0)),
                  pl.BlockSpec((tm, N_BRANCH * BR_W), lambda i: (i, 0))] + [full(w) for w in ws],
        out_specs=pl.BlockSpec((tm, D_MODEL), lambda i: (i, 0)),
        out_shape=jax.ShapeDtypeStruct((T, D_MODEL), f32),
        compiler_params=_cparams("parallel"),
        name="merge",
    )(x, br, *ws)


ROUTE_TT = 128
_STAIR = [(k1, k2) for k1 in range(P_TOPK) for k2 in range(P_TOPK // (k1 + 1))]
N_CAND = -(-len(_STAIR) // 8) * 8


def _cand_tables():
    p12 = np.zeros((N_CAND, 2 * P_TOPK), np.float32)
    pos = np.full((N_CAND, LANE), 4.0 * P_TOPK * P_TOPK, np.float32)
    bias = np.full((N_CAND, LANE), NEG_INF, np.float32)
    for r, (k1, k2) in enumerate(_STAIR):
        p12[r, k1] = 1.0
        p12[r, P_TOPK + k2] = 1.0
        pos[r, :] = k1 * P_TOPK + k2
        bias[r, :] = 0.0
    return jnp.asarray(p12), jnp.asarray(pos), jnp.asarray(bias)


def _route_body(q_ref, keys_ref, p12_ref, cpos_ref, cbias_ref, r2_ref, e2_ref, lim_ref, cw_ref, sc_sc, v_sc):
    tt = q_ref.shape[1]
    rowf = lax.broadcasted_iota(i32, (P_NKEYS, tt), 0).astype(f32)
    k1row = lax.broadcasted_iota(i32, (P_TOPK, 1), 0).astype(f32)
    cpos = cpos_ref[...]

    for g in range(2 * P_HEADS):
        sc_sc[g] = lax.dot_general(keys_ref[g], q_ref[g].astype(bf16), _NT, preferred_element_type=f32)

    def run(tie_safe):
        bad = jnp.zeros((1, tt), f32)
        A = B = C = None
        for step in range(P_HEADS + 2):
            if step < P_HEADS:
                A = dict(h=step, work=sc_sc[pl.ds(2 * step, 2)],
                         rank=jnp.full((2, P_NKEYS, tt), float(P_TOPK), f32))
            else:
                A = None
            for k in range(P_TOPK):
                if A is not None:
                    work = A["work"]
                    m = jnp.max(work, axis=1, keepdims=True)
                    if tie_safe:
                        idx = jnp.min(jnp.where(work == m, rowf[None], float(P_NKEYS)), axis=1, keepdims=True)
                        hit = rowf[None] == idx
                    else:
                        hit = work == m
                    A["rank"] = jnp.where(hit, float(k), A["rank"])
                    A["work"] = jnp.where(hit, NEG_INF, work)
                    v_sc[A["h"] % 2, 0, k:k + 1, :] = m[0]
                    v_sc[A["h"] % 2, 1, k:k + 1, :] = m[1]
                if B is not None:
                    cand = B["cand"]
                    m = jnp.max(cand, axis=0, keepdims=True)
                    ps = jnp.min(jnp.where(cand == m, cpos, 4.0 * P_TOPK * P_TOPK), axis=0, keepdims=True)
                    B["cand"] = jnp.where(cpos == ps, NEG_INF, cand)
                    B["cnt"] = B["cnt"] + jnp.where(k1row == jnp.floor(ps * (1.0 / P_TOPK)), 1.0, 0.0)
                    if k == 0:
                        B["top0"] = m
                        B["zsum"] = jnp.ones_like(m)
                    else:
                        B["zsum"] = B["zsum"] + jnp.exp(m - B["top0"])
                if C is not None:
                    C["lim"] = jnp.where(C["rank1"] == float(k), C["cnt"][k:k + 1, :], C["lim"])
            if C is not None:
                h = C["h"]
                lim_ref[h, 0] = C["lim"]
                cw_ref[h, 0] = jnp.exp(sc_sc[2 * h] - C["top1"]) / C["zsum"]
            C = None
            if B is not None:
                C = dict(h=B["h"], rank1=B["rank1"], cnt=B["cnt"], zsum=B["zsum"], top1=B["top1"],
                         lim=jnp.zeros((P_NKEYS, tt), f32))
            B = None
            if A is not None:
                h = A["h"]
                if not tie_safe:
                    removed = jnp.sum(jnp.where(A["rank"] < float(P_TOPK), 1.0, 0.0), axis=1)
                    bad = jnp.maximum(bad, jnp.max(jnp.where(removed == float(P_TOPK), 0.0, 1.0), axis=0, keepdims=True))
                v1 = v_sc[h % 2, 0]
                v2 = v_sc[h % 2, 1]
                r2_ref[h] = A["rank"][1].astype(bf16)
                e2_ref[h] = jnp.exp(sc_sc[2 * h + 1] - v2[0:1, :]).astype(bf16)
                cand = (jnp.dot(p12_ref[:, 0:P_TOPK], v1, precision=_HI, preferred_element_type=f32)
                        + jnp.dot(p12_ref[:, P_TOPK:2 * P_TOPK], v2, precision=_HI, preferred_element_type=f32)
                        + cbias_ref[...])
                B = dict(h=h, cand=cand, cnt=jnp.zeros((P_TOPK, tt), f32), rank1=A["rank"][0], top1=v1[0:1, :])
        return bad

    bad = run(False)

    @pl.when(jnp.max(bad) > 0.0)
    def _exact_ties():
        run(True)


def _route(q16, keys):
    G, T, _ = q16.shape
    tt = ROUTE_TT
    tabs = _cand_tables()
    spec = pl.BlockSpec((P_HEADS, P_NKEYS, tt), lambda i: (0, 0, i))
    full = lambda a: pl.BlockSpec(a.shape, lambda i: (0,) * a.ndim)
    return pl.pallas_call(
        _route_body,
        grid=(T // tt,),
        in_specs=[pl.BlockSpec((G, tt, P_HALF), lambda i: (0, i, 0)), full(keys)] + [full(t) for t in tabs],
        out_specs=[spec] * 2 + [pl.BlockSpec((P_HEADS, 1, P_NKEYS, tt), lambda i: (0, i, 0, 0))] * 2,
        out_shape=[jax.ShapeDtypeStruct((P_HEADS, P_NKEYS, T), bf16)] * 2
                  + [jax.ShapeDtypeStruct((P_HEADS, T // tt, P_NKEYS, tt), f32)] * 2,
        scratch_shapes=[pltpu.VMEM((2 * P_HEADS, P_NKEYS, tt), f32), pltpu.VMEM((2, 2, P_TOPK, tt), f32)],
        compiler_params=_cparams("parallel"),
        name="peer_route",
    )(q16, keys, *tabs)


PEER_NA = 4
PEER_EB = PEER_NA * P_NKEYS
BF16_ROWS = 16
F32_ROWS = 8


def _peer_weights(act, row0, r2_ref, e2_ref, lim_ref, cw_ref):
    tt = act.shape[1]
    grp = (P_NKEYS // BF16_ROWS, BF16_ROWS, tt)
    zero = jnp.zeros((), bf16)
    out = []
    for al in range(PEER_NA):
        gate = None
        for h in range(P_HEADS):
            r = row0 + al
            rep = lambda ref, j: jnp.broadcast_to(ref[h, j, r:r + 1, :], (F32_ROWS, LANE))
            lim = jnp.concatenate([rep(lim_ref, j) for j in range(tt // LANE)], axis=1)
            cw = jnp.concatenate([rep(cw_ref, j) for j in range(tt // LANE)], axis=1)
            lim = jnp.concatenate([lim, lim], axis=0).astype(bf16)
            cw = jnp.concatenate([cw, cw], axis=0).astype(bf16)
            t = jnp.where(r2_ref[h].reshape(grp) < lim[None], e2_ref[h].reshape(grp), zero) * cw[None]
            gate = t if gate is None else gate + t
        a = act[al * P_NKEYS:(al + 1) * P_NKEYS, :]
        gelu = (0.5 * a * (1.0 + lax.erf(a * (2.0 ** -0.5)))).astype(bf16)
        out.append((gelu.reshape(grp) * gate).reshape(P_NKEYS, tt))
    return jnp.concatenate(out, axis=0)


PEER_BPS = 4


def _peer_body(x_ref, u0_ref, u1_ref, u2_ref, u3_ref, un_ref, vt0_ref, vt1_ref, vt2_ref, vt3_ref,
               r2_ref, e2_ref, lim_ref, cw_ref, g_ref, b_ref, o_ref, xb_sc, acc_sc, actx_sc, acty_sc):
    s = pl.program_id(1)
    route = (r2_ref, e2_ref, lim_ref, cw_ref)

    @pl.when(s == 0)
    def _init():
        xb0 = x_ref[...].astype(bf16)
        xb_sc[...] = xb0
        acc_sc[...] = jnp.zeros_like(acc_sc)
        actx_sc[...] = lax.dot_general(u0_ref[...], xb0, _NT, preferred_element_type=f32)

    xb = xb_sc[...]
    first = lambda u_ref: lax.dot_general(u_ref[...], xb, _NT, preferred_element_type=f32)
    act1 = first(u1_ref)
    acty_sc[...] = first(u2_ref)
    w0 = _peer_weights(actx_sc[...], 0, *route)
    acc_sc[...] += jnp.dot(vt0_ref[...], w0, preferred_element_type=f32)
    w1 = _peer_weights(act1, PEER_NA, *route)
    acc_sc[...] += jnp.dot(vt1_ref[...], w1, preferred_element_type=f32)
    act3 = first(u3_ref)
    actx_sc[...] = first(un_ref)
    w2 = _peer_weights(acty_sc[...], 2 * PEER_NA, *route)
    acc_sc[...] += jnp.dot(vt2_ref[...], w2, preferred_element_type=f32)
    w3 = _peer_weights(act3, 3 * PEER_NA, *route)
    acc_sc[...] += jnp.dot(vt3_ref[...], w3, preferred_element_type=f32)

    @pl.when(s == pl.num_programs(1) - 1)
    def _finish():
        y = acc_sc[...].T
        o_ref[...] = _layernorm(DN_ALPHA * x_ref[...] + y, g_ref[...], b_ref[...])


def _peer(x, route, W, layer, tt):
    T = x.shape[0]
    r2, e2, lim, cw = route
    nblk = P_EXPERTS // PEER_EB
    ns = nblk // PEER_BPS
    tok = pl.BlockSpec((P_HEADS, P_NKEYS, tt), lambda i, s: (0, 0, i))
    per_a = pl.BlockSpec((P_HEADS, tt // LANE, PEER_BPS * PEER_NA, LANE), lambda i, s: (0, i, s, 0))
    vec = pl.BlockSpec((None, 1, D_MODEL), lambda i, s: (layer, 0, 0))
    u_blk = lambda k: pl.BlockSpec((None, PEER_EB, D_MODEL),
                                   lambda i, s: (layer, jnp.minimum(PEER_BPS * s + k, nblk - 1), 0))
    vt_blk = lambda k: pl.BlockSpec((None, D_MODEL, PEER_EB), lambda i, s: (layer, 0, PEER_BPS * s + k))
    return pl.pallas_call(
        _peer_body,
        grid=(T // tt, ns),
        in_specs=[pl.BlockSpec((tt, D_MODEL), lambda i, s: (i, 0))]
                 + [u_blk(k) for k in range(PEER_BPS + 1)] + [vt_blk(k) for k in range(PEER_BPS)]
                 + [tok, tok, per_a, per_a, vec, vec],
        out_specs=pl.BlockSpec((tt, D_MODEL), lambda i, s: (i, 0)),
        out_shape=jax.ShapeDtypeStruct((T, D_MODEL), f32),
        scratch_shapes=[pltpu.VMEM((tt, D_MODEL), bf16), pltpu.VMEM((D_MODEL, tt), f32),
                        pltpu.VMEM((PEER_EB, tt), f32), pltpu.VMEM((PEER_EB, tt), f32)],
        compiler_params=_cparams("parallel", "arbitrary"),
        name="peer_experts",
    )(x, *([W["p_u"]] * (PEER_BPS + 1)), *([W["p_vt"]] * PEER_BPS), r2, e2, lim, cw, W["ln2_g"], W["ln2_b"])


_IN_SIZES = (512, 512, 512, 512, 4, 4, 256, 256, 512, 512, 16, 512, 768, 8, 3072)


def _prep_small(m_i_bias, m_f_bias, m_norm, g_a_up, g_a_bias, g_norm,
                s_conv_w, s_conv_b, s_dt_bias, s_A_log, s_D, s_norm):
    zl = lambda n: jnp.zeros((n,), f32)
    row = lambda a: a.astype(f32).reshape(1, -1)
    return dict(
        bias_s=row(jnp.concatenate([m_i_bias, m_f_bias, zl(G_RANK), s_dt_bias, zl(LANE - SM_DT - S_HEADS)])),
        alog=row(jnp.concatenate([zl(SM_DT), s_A_log, zl(LANE - SM_DT - S_HEADS)])),
        mnorm=row(m_norm), gnorm=row(g_norm), snorm=row(s_norm),
        drow=row(jnp.repeat(s_D, S_HD)),
        gup=jnp.zeros((LANE, G_KW), f32).at[SM_GA:SM_GA + G_RANK].set(g_a_up).astype(bf16),
        gab=row(g_a_bias),
        cw=s_conv_w.astype(f32), cb=row(s_conv_b),
    )


def _prep_big(w_in, w_branch, w_out, ln1_g, ln1_b, p_wq, p_keys, p_u, p_v, ln2_g, ln2_b):
    offs = [0]
    for s in _IN_SIZES:
        offs.append(offs[-1] + s)
    colw = lambda i: w_in[:, :, offs[i]:offs[i + 1]]
    (mq, mk, mv, mo, mi, mf, gq, gk, gv, gr, ga, sz, sxbc, sdt, gate) = [colw(i) for i in range(15)]
    pad = jnp.zeros((DEPTH, D_MODEL, LANE - (SM_DT + S_HEADS)), w_in.dtype)
    w1 = jnp.concatenate([mq, mk, mv, mo, gq, gk, gv, gr, sz, sxbc, mi, mf, ga, sdt, pad], axis=2)
    rows = lambda a: a.astype(f32).reshape(DEPTH, 1, -1)
    return dict(
        w1=w1.astype(bf16), w_gate=gate.astype(bf16),
        w_branch=w_branch.astype(bf16), w_out=w_out.astype(bf16),
        ln1_g=rows(ln1_g), ln1_b=rows(ln1_b),
        p_wq=p_wq.astype(bf16),
        p_keys=p_keys.astype(bf16).reshape(DEPTH, 2 * P_HEADS, P_NKEYS, P_HALF),
        p_u=p_u.astype(bf16), p_vt=jnp.swapaxes(p_v.astype(bf16), 1, 2),
        ln2_g=rows(ln2_g), ln2_b=rows(ln2_b),
    )


def _run_layer(x, states, slab, W, Ws, layer, c, valid, nb, tm, peer_tt):
    B, L, _ = x.shape
    nchunks = L // valid
    Lp = nchunks * c
    if c != valid:
        xp = jnp.pad(x.reshape(B, nchunks, valid, D_MODEL), ((0, 0), (0, 0), (0, c - valid), (0, 0)))
        xp = xp.reshape(B * Lp, D_MODEL)
    else:
        xp = x.reshape(B * L, D_MODEL)
    z = _matmul(xp, W["w1"], layer, tm).reshape(B, Lp, Z_W)
    br, new_states = _mixer(z, states, slab, Ws, c, valid, nb)
    if c != valid:
        br = br.reshape(B, nchunks, c, N_BRANCH * BR_W)[:, :, :valid]
    xt = x.reshape(B * L, D_MODEL)
    h = _merge(xt, br.reshape(B * L, N_BRANCH * BR_W), W, layer, tm)
    q16 = _wq_matmul(h, W["p_wq"], layer, tm)
    route = _route(q16, W["p_keys"][layer])
    y = _peer(h, route, W, layer, peer_tt)
    return y.reshape(B, L, D_MODEL), new_states


def _zero_states(b):
    return (jnp.zeros((1, b, M_HEADS, M_HD, M_HD), f32), jnp.zeros((1, b, M_HEADS, M_HD), f32),
            jnp.zeros((1, b, M_HEADS), f32), jnp.zeros((1, b, G_HEADS, G_DK, G_DV), f32),
            jnp.zeros((1, b, S_HEADS, S_HD, S_STATE), f32), jnp.zeros((1, b, S_CONV - 1, S_XBC), f32))


PROMPT_CHUNK = 128
SAMPLE_CHUNK = 16
PROMPT_NB = 2
SAMPLE_NB = 8


def kernel(x_prompt, x_sample, state_mlstm_C, state_mlstm_n, state_mlstm_m, state_gla_S, state_ssm_h, state_conv, w_in, m_i_bias, m_f_bias, m_norm, g_a_up, g_a_bias, g_norm, s_conv_w, s_conv_b, s_dt_bias, s_A_log, s_D, s_norm, w_branch, w_out, ln1_g, ln1_b, p_wq, p_keys, p_u, p_v, ln2_g, ln2_b):
    small = (m_i_bias, m_f_bias, m_norm, g_a_up, g_a_bias, g_norm,
             s_conv_w, s_conv_b, s_dt_bias, s_A_log, s_D, s_norm)
    W = _prep_big(w_in, w_branch, w_out, ln1_g, ln1_b, p_wq, p_keys, p_u, p_v, ln2_g, ln2_b)
    in_states = (state_mlstm_C, state_mlstm_n, state_mlstm_m, state_gla_S, state_ssm_h, state_conv)
    hp, hs = x_prompt, x_sample
    ls = x_sample.shape[1]
    new_p = [[] for _ in range(6)]
    new_s = [[] for _ in range(6)]
    for l in range(DEPTH):
        Ws = _prep_small(*[w[l] for w in small])
        hp, sp = _run_layer(hp, _zero_states(hp.shape[0]), 0, W, Ws, l, PROMPT_CHUNK, PROMPT_CHUNK, PROMPT_NB, 256, 512)
        hs, ss = _run_layer(hs, in_states, l, W, Ws, l, SAMPLE_CHUNK, ls, SAMPLE_NB, 256, 512)
        for j in range(6):
            new_p[j].append(sp[j])
            new_s[j].append(ss[j])
    P = [jnp.stack(a, axis=0) for a in new_p]
    S = [jnp.stack(a, axis=0) for a in new_s]
    return (hp, hs, P[0], P[1], P[2], P[3], P[4], P[5], S[0], S[1], S[2], S[3], S[4], S[5])
```

```python
import functools
import math

import numpy as np
import jax
import jax.numpy as jnp
from jax import lax
from jax.experimental import pallas as pl
from jax.experimental.pallas import tpu as pltpu

f32, bf16, i32 = jnp.float32, jnp.bfloat16, jnp.int32
NEG_INF = float("-inf")

D_MODEL = 1024
DEPTH = 2
BR_W = 512
N_BRANCH = 3
M_HEADS, M_HD = 4, 128
G_HEADS, G_DK, G_DV = 4, 64, 128
G_KW, G_VW = G_HEADS * G_DK, G_HEADS * G_DV
G_RANK = 16
G_TAU = 16.0
S_HD, S_HEADS, S_GROUPS, S_STATE, S_CONV = 64, 8, 2, 64, 4
S_HG = S_HEADS // S_GROUPS
S_W = S_HEADS * S_HD
S_XBC = S_W + 2 * S_GROUPS * S_STATE
P_HEADS, P_NKEYS, P_HALF, P_TOPK = 8, 128, 128, 16
P_EXPERTS = P_NKEYS * P_NKEYS
DN_ALPHA = (2.0 * DEPTH) ** 0.25
EPS = 1e-5

LANE = 128

Z_MQ, Z_MK, Z_MV, Z_MO = 0, 512, 1024, 1536
Z_GQ, Z_GK, Z_GV, Z_GR = 2048, 2304, 2560, 3072
Z_SZ, Z_XBC, Z_SMALL = 3584, 4096, 4864
Z_W = Z_SMALL + LANE
SM_MI, SM_MF, SM_GA, SM_DT = 0, 4, 8, 24

VMEM_LIMIT = 56 * 1024 * 1024

_NT = (((1,), (1,)), ((), ()))
_TN = (((0,), (0,)), ((), ()))
_HI = lax.Precision.HIGHEST


def _cparams(*sem):
    return pltpu.CompilerParams(dimension_semantics=sem, vmem_limit_bytes=VMEM_LIMIT)


def _mm_body(x_ref, w_ref, o_ref):
    o_ref[...] = jnp.dot(x_ref[...].astype(bf16), w_ref[...], preferred_element_type=f32)


def _matmul(x, w, layer, tm):
    T, K = x.shape
    N = w.shape[2]
    return pl.pallas_call(
        _mm_body,
        grid=(T // tm,),
        in_specs=[pl.BlockSpec((tm, K), lambda i: (i, 0)), pl.BlockSpec((None, K, N), lambda i: (layer, 0, 0))],
        out_specs=pl.BlockSpec((tm, N), lambda i: (i, 0)),
        out_shape=jax.ShapeDtypeStruct((T, N), f32),
        compiler_params=_cparams("parallel"),
        name="in_proj",
    )(x, w)


def _wq_body(x_ref, w_ref, o_ref):
    r = jnp.dot(x_ref[...].astype(bf16), w_ref[...], preferred_element_type=f32)
    for g in range(2 * P_HEADS):
        o_ref[g] = r[:, g * P_HALF:(g + 1) * P_HALF]


def _wq_matmul(x, w, layer, tm):
    T, K = x.shape
    G = 2 * P_HEADS
    return pl.pallas_call(
        _wq_body,
        grid=(T // tm,),
        in_specs=[pl.BlockSpec((tm, K), lambda i: (i, 0)),
                  pl.BlockSpec((None, K, G * P_HALF), lambda i: (layer, 0, 0))],
        out_specs=pl.BlockSpec((G, tm, P_HALF), lambda i: (0, i, 0)),
        out_shape=jax.ShapeDtypeStruct((G, T, P_HALF), f32),
        compiler_params=_cparams("parallel"),
        name="peer_query",
    )(x, w)


def _mixer_body(c, valid, nb,
                z_ref, C0, n0, m0, S0, h0, buf0,
                bias_s, alog, mnorm, gnorm, snorm, drow, gup, gab, cw, cb,
                br_ref, C1, n1, m1, S1, h1, buf1,
                C_sc, n_sc, m_sc, S_sc, h_sc, xp_sc):
    j = pl.program_id(1)

    @pl.when(j == 0)
    def _load_state():
        C_sc[...] = C0[...]
        n_sc[...] = n0[...]
        for b in range(nb):
            for h in range(M_HEADS):
                m_sc[b, h:h + 1, :] = jnp.broadcast_to(m0[b, :, h:h + 1], (1, LANE))
        S_sc[...] = S0[...]
        h_sc[...] = h0[...]
        xp_sc[:, 5:8, :] = buf0[...]

    row = lax.broadcasted_iota(i32, (c, c), 0)
    col = lax.broadcasted_iota(i32, (c, c), 1)
    causal = row >= col
    tri = causal.astype(f32)
    lane = lax.broadcasted_iota(i32, (1, LANE), 1)
    tcol = lax.broadcasted_iota(i32, (c, 1), 0)
    tvalid = tcol < valid
    is_lf = (lane >> 2) == (SM_MF >> 2)
    is_dt = (lane >> 3) == (SM_DT >> 3)
    a_row = jnp.where(is_dt, -jnp.exp(alog[...]), 0.0)
    eye_f = (lax.broadcasted_iota(i32, (LANE, LANE), 0) == lax.broadcasted_iota(i32, (LANE, LANE), 1)).astype(f32)
    eye_b = eye_f.astype(bf16)
    tvalid_r = lax.broadcasted_iota(i32, (1, c), 1) < valid

    def xpose_b(x):
        n = x.shape[1]
        return lax.dot_general(eye_b[:n, :n], x, _NT, preferred_element_type=f32).astype(bf16)

    def xpose_f(x):
        return lax.dot_general(eye_f, x, _NT, precision=_HI, preferred_element_type=f32)


    def prologue(b):
        zs = z_ref[b, :, Z_SMALL:Z_SMALL + LANE]
        small = zs + bias_s[...]
        LI = jnp.where(tvalid, small, NEG_INF)
        LF = jnp.where(tvalid, jnp.where(is_lf, jax.nn.log_sigmoid(small), 0.0), 0.0)
        DT = jnp.where(tvalid, jnp.where(is_dt, jax.nn.softplus(small), 0.0), 0.0)
        cum = jnp.dot(tri, LF + DT * a_row, precision=_HI, preferred_element_type=f32)
        smallT = xpose_f(small)
        LIT = jnp.where(tvalid_r, smallT[SM_MI:SM_MI + F32_ROWS, :], NEG_INF)
        DTT = jnp.where(tvalid_r, jax.nn.softplus(smallT[SM_DT:SM_DT + S_HEADS, :]), 0.0)
        return dict(zs=zs, LI=LI, DT=DT, cum=cum, cumT=xpose_f(cum), LIT=LIT, DTT=DTT)

    def mlstm_head(b, h, P):
        sl = slice(h * M_HD, (h + 1) * M_HD)
        q = z_ref[b, :, Z_MQ + h * M_HD:Z_MQ + (h + 1) * M_HD] * (M_HD ** -0.5)
        k = z_ref[b, :, Z_MK + h * M_HD:Z_MK + (h + 1) * M_HD]
        v = z_ref[b, :, Z_MV + h * M_HD:Z_MV + (h + 1) * M_HD]
        qb, kb, vb = q.astype(bf16), k.astype(bf16), v.astype(bf16)
        Cm_ = C_sc[b, h]
        qk = lax.dot_general(qb, kb, _NT, preferred_element_type=f32)
        qC = lax.dot_general(qb, Cm_.astype(bf16), _NT, preferred_element_type=f32)
        yield
        b_c = P["cum"][:, SM_MF + h:SM_MF + h + 1]
        b_r = P["cumT"][SM_MF + h:SM_MF + h + 1, :]
        li_c = P["LI"][:, SM_MI + h:SM_MI + h + 1]
        li_r = P["LIT"][SM_MI + h:SM_MI + h + 1, :]
        m_prev = m_sc[b, h:h + 1, 0:1]
        a = b_c + m_prev
        d = jnp.where(causal, b_c - b_r + li_r, NEG_INF)
        m_t = jnp.maximum(a, jnp.max(d, axis=1, keepdims=True))
        s = qk * jnp.exp(d - m_t)
        e_in = jnp.exp(a - m_t)
        sv = jnp.dot(s.astype(bf16), vb, preferred_element_type=f32)
        b_last = b_c[c - 1:c, :]
        g_c = b_last - b_c + li_c
        m_new = jnp.maximum(b_last + m_prev, jnp.max(g_c, axis=0, keepdims=True))
        e_c = jnp.exp(b_last + m_prev - m_new)
        wg = jnp.exp(g_c - m_new)
        upd = jnp.dot(xpose_b((wg * v).astype(bf16)), kb, preferred_element_type=f32)
        yield
        num = sv + e_in * qC
        n_row = n_sc[b, h:h + 1, :]
        den = jnp.sum(s, axis=1, keepdims=True) + e_in * jnp.sum(q * n_row, axis=1, keepdims=True)
        hh = num / jnp.maximum(jnp.abs(den), jnp.exp(-m_t))
        C_sc[b, h] = e_c * Cm_ + upd
        n_sc[b, h:h + 1, :] = e_c * n_row + jnp.sum(wg * k, axis=0, keepdims=True)
        m_sc[b, h:h + 1, :] = jnp.broadcast_to(m_new, (1, LANE))
        mu = jnp.mean(hh, axis=1, keepdims=True)
        xc = hh - mu
        hn = xc * lax.rsqrt(jnp.mean(xc * xc, axis=1, keepdims=True) + EPS)
        mo = z_ref[b, :, Z_MO + h * M_HD:Z_MO + (h + 1) * M_HD]
        br_ref[b, :, sl] = hn * mnorm[:, sl] * jax.nn.sigmoid(mo)

    def gla_seq(b, P):
        ga = jnp.dot(P["zs"].astype(bf16), gup[...], preferred_element_type=f32) + gab[...]
        yield
        log_a = jnp.where(tvalid, jax.nn.log_sigmoid(ga) * (1.0 / G_TAU), 0.0)
        lam = jnp.dot(tri, log_a, precision=_HI, preferred_element_type=f32)
        yield
        levels = []
        m_half = c // 2
        while m_half >= 1:
            sh = int(math.log2(2 * m_half))
            if 2 * m_half >= F32_ROWS:
                blk = lam.reshape(c // (2 * m_half), 2 * m_half, G_KW)[:, m_half - 1:m_half, :]
                beta = jnp.broadcast_to(blk, (c // (2 * m_half), 2 * m_half, G_KW)).reshape(c, G_KW)
            else:
                sel = (col == ((row >> sh) << sh) + (m_half - 1)).astype(f32)
                beta = jnp.dot(sel, lam, precision=_HI, preferred_element_type=f32)
            levels.append((m_half, sh, beta))
            m_half //= 2
        dec_c = [jnp.exp(xpose_f(lam[c - F32_ROWS:c, i * LANE:(i + 1) * LANE])[:, F32_ROWS - 1:F32_ROWS])
                 for i in range(G_KW // LANE)]
        gq = z_ref[b, :, Z_GQ:Z_GQ + G_KW] * (G_DK ** -0.5)
        gk = z_ref[b, :, Z_GK:Z_GK + G_KW]
        yield
        att = [None] * G_HEADS
        for m_half, sh, beta in levels:
            upper = (tcol & (2 * m_half - 1)) >= m_half
            qm = (gq * jnp.exp(jnp.where(upper, lam - beta, NEG_INF))).astype(bf16)
            km = (gk * jnp.exp(jnp.where(upper, NEG_INF, beta - lam))).astype(bf16)
            ps = [lax.dot_general(qm[:, h * G_DK:(h + 1) * G_DK], km[:, h * G_DK:(h + 1) * G_DK], _NT,
                                  preferred_element_type=f32) for h in range(G_HEADS)]
            yield
            if 2 * m_half < c:
                same_blk = (row >> sh) == (col >> sh)
                ps = [jnp.where(same_blk, p, 0.0) for p in ps]
            att = [p if a_ is None else a_ + p for a_, p in zip(att, ps)]
        q_in = (gq * jnp.exp(lam)).astype(bf16)
        k_out = (gk * jnp.exp(lam[c - 1:c, :] - lam)).astype(bf16)
        k_outT = [xpose_b(k_out[:, i * LANE:(i + 1) * LANE]) for i in range(G_KW // LANE)]
        hpl = LANE // G_DK
        for h in range(G_HEADS):
            ks = slice(h * G_DK, (h + 1) * G_DK)
            vs = slice(h * G_DV, (h + 1) * G_DV)
            v = z_ref[b, :, Z_GV + h * G_DV:Z_GV + (h + 1) * G_DV]
            vb = v.astype(bf16)
            S_ = S_sc[b, h]
            o_att = jnp.dot(att[h].astype(bf16), vb, preferred_element_type=f32)
            o_st = jnp.dot(q_in[:, ks], S_.astype(bf16), preferred_element_type=f32)
            rs = slice((h % hpl) * G_DK, (h % hpl + 1) * G_DK)
            upd = jnp.dot(k_outT[h // hpl][rs, :], vb, preferred_element_type=f32)
            yield
            diag = jnp.sum(gq[:, ks] * gk[:, ks], axis=1, keepdims=True)
            o = o_att + diag * v + o_st
            S_sc[b, h] = dec_c[h // hpl][rs, :] * S_ + upd
            on = o * lax.rsqrt(jnp.mean(o * o, axis=1, keepdims=True) + EPS)
            gr = z_ref[b, :, Z_GR + h * G_DV:Z_GR + (h + 1) * G_DV]
            br_ref[b, :, BR_W + h * G_DV:BR_W + (h + 1) * G_DV] = on * gnorm[:, vs] * jax.nn.silu(gr)

    def ssd_group(b, g, P, act):
        Bm = act[:, S_W + g * S_STATE:S_W + (g + 1) * S_STATE].astype(bf16)
        c0 = S_W + S_GROUPS * S_STATE + g * S_STATE
        Cmat = act[:, c0:c0 + S_STATE].astype(bf16)
        cbm = lax.dot_general(Cmat, Bm, _NT, preferred_element_type=f32)
        yield
        ys = []
        ssq = None
        for hg in range(S_HG):
            hh_ = g * S_HG + hg
            ps = slice(hh_ * S_HD, (hh_ + 1) * S_HD)
            xs = act[:, ps]
            dt_c = P["DT"][:, SM_DT + hh_:SM_DT + hh_ + 1]
            dt_r = P["DTT"][hh_:hh_ + 1, :]
            l_c = P["cum"][:, SM_DT + hh_:SM_DT + hh_ + 1]
            l_r = P["cumT"][SM_DT + hh_:SM_DT + hh_ + 1, :]
            decay = jnp.exp(jnp.where(causal, l_c - l_r, NEG_INF))
            w = (cbm * decay * dt_r).astype(bf16)
            hs_ = h_sc[b, hh_]
            l_last = l_c[c - 1:c, :]
            ws = jnp.exp(l_last - l_c) * dt_c
            y_in = jnp.dot(w, xs.astype(bf16), preferred_element_type=f32)
            y_st = lax.dot_general(Cmat, hs_.astype(bf16), _NT, preferred_element_type=f32)
            upd = jnp.dot(xpose_b((ws * xs).astype(bf16)), Bm, preferred_element_type=f32)
            yield
            y = y_in + y_st * jnp.exp(l_c)
            h_sc[b, hh_] = jnp.exp(l_last) * hs_ + upd
            yy = (y + drow[:, ps] * xs) * jax.nn.silu(z_ref[b, :, Z_SZ + hh_ * S_HD:Z_SZ + (hh_ + 1) * S_HD])
            ys.append(yy)
            sq = jnp.sum(yy * yy, axis=1, keepdims=True)
            ssq = sq if ssq is None else ssq + sq
        scale = lax.rsqrt(ssq * (1.0 / (S_W // S_GROUPS)) + EPS)
        for hg in range(S_HG):
            hh_ = g * S_HG + hg
            ps = slice(hh_ * S_HD, (hh_ + 1) * S_HD)
            br_ref[b, :, 2 * BR_W + hh_ * S_HD:2 * BR_W + (hh_ + 1) * S_HD] = ys[hg] * scale * snorm[:, ps]

    streams = []
    for b in range(nb):
        P = prologue(b)
        xp_sc[b, 8:8 + c, :] = z_ref[b, :, Z_XBC:Z_XBC + S_XBC]
        conv = cb[...] + cw[0:1, :] * xp_sc[b, pl.ds(5, c), :]
        for jj in range(1, S_CONV):
            conv = conv + cw[jj:jj + 1, :] * xp_sc[b, pl.ds(5 + jj, c), :]
        new_buf = xp_sc[b, pl.ds(5 + valid, S_CONV - 1), :]
        xp_sc[b, 5:8, :] = new_buf
        act = jax.nn.silu(conv)
        streams += [mlstm_head(b, h, P) for h in range(M_HEADS)]
        streams += [gla_seq(b, P)]
        streams += [ssd_group(b, g, P, act) for g in range(S_GROUPS)]
    while streams:
        alive = []
        for st in streams:
            try:
                next(st)
                alive.append(st)
            except StopIteration:
                pass
        streams = alive

    @pl.when(j == pl.num_programs(1) - 1)
    def _store_state():
        C1[...] = C_sc[...]
        n1[...] = n_sc[...]
        m1[...] = m_sc[...]
        S1[...] = S_sc[...]
        h1[...] = h_sc[...]
        buf1[...] = xp_sc[:, 5:8, :]


def _mixer(z, states, layer, W, c, valid, nb):
    B, L, _ = z.shape
    nc = L // c
    C0, n0, m0, S0, h0, buf0 = states
    m0 = m0.reshape(m0.shape[0], B, 1, M_HEADS)

    def per_b(shape):
        nd = len(shape)
        return pl.BlockSpec((nb,) + shape, lambda b, j: (b,) + (0,) * nd)

    def per_b_in(shape):
        nd = len(shape)
        return pl.BlockSpec((None, nb) + shape, lambda b, j: (layer, b) + (0,) * nd)

    def const(shape):
        nd = len(shape)
        return pl.BlockSpec(shape, lambda b, j: (0,) * nd)

    small_w = (W["bias_s"], W["alog"], W["mnorm"], W["gnorm"], W["snorm"], W["drow"],
               W["gup"], W["gab"], W["cw"], W["cb"])
    st_shapes = [(M_HEADS, M_HD, M_HD), (M_HEADS, M_HD), (M_HEADS, LANE),
                 (G_HEADS, G_DK, G_DV), (S_HEADS, S_HD, S_STATE), (S_CONV - 1, S_XBC)]
    out = pl.pallas_call(
        functools.partial(_mixer_body, c, valid, nb),
        grid=(B // nb, nc),
        in_specs=[pl.BlockSpec((nb, c, Z_W), lambda b, j: (b, j, 0)),
                  per_b_in((M_HEADS, M_HD, M_HD)), per_b_in((M_HEADS, M_HD)), per_b_in((1, M_HEADS)),
                  per_b_in((G_HEADS, G_DK, G_DV)), per_b_in((S_HEADS, S_HD, S_STATE)),
                  per_b_in((S_CONV - 1, S_XBC))]
                 + [const(w.shape) for w in small_w],
        out_specs=[pl.BlockSpec((nb, c, N_BRANCH * BR_W), lambda b, j: (b, j, 0))]
                  + [per_b(s) for s in st_shapes],
        out_shape=[jax.ShapeDtypeStruct((B, L, N_BRANCH * BR_W), f32)]
                  + [jax.ShapeDtypeStruct((B,) + s, f32) for s in st_shapes],
        scratch_shapes=[pltpu.VMEM((nb, M_HEADS, M_HD, M_HD), f32), pltpu.VMEM((nb, M_HEADS, M_HD), f32),
                        pltpu.VMEM((nb, M_HEADS, LANE), f32), pltpu.VMEM((nb, G_HEADS, G_DK, G_DV), f32),
                        pltpu.VMEM((nb, S_HEADS, S_HD, S_STATE), f32), pltpu.VMEM((nb, c + 8, S_XBC), f32)],
        compiler_params=_cparams("parallel", "arbitrary"),
        name="mixer",
    )(z, C0, n0, m0, S0, h0, buf0, *small_w)
    br, C1, n1, m1, S1, h1, buf1 = out
    return br, (C1, n1, m1[:, :, 0], S1, h1, buf1)


def _layernorm(x, g, b):
    mu = jnp.mean(x, axis=1, keepdims=True)
    xc = x - mu
    return xc * lax.rsqrt(jnp.mean(xc * xc, axis=1, keepdims=True) + EPS) * g + b


def _merge_body(x_ref, br_ref, wg_ref, wb_ref, wo_ref, g_ref, b_ref, o_ref):
    x = x_ref[...]
    xb = x.astype(bf16)
    mixed = None
    for n in range(N_BRANCH):
        gate = jnp.dot(xb, wg_ref[:, n * D_MODEL:(n + 1) * D_MODEL], preferred_element_type=f32)
        proj = jnp.dot(br_ref[:, n * BR_W:(n + 1) * BR_W].astype(bf16), wb_ref[n], preferred_element_type=f32)
        t = jax.nn.sigmoid(gate) * proj
        mixed = t if mixed is None else mixed + t
    y = jnp.dot(mixed.astype(bf16), wo_ref[...], preferred_element_type=f32)
    o_ref[...] = _layernorm(DN_ALPHA * x + y, g_ref[...], b_ref[...])


def _merge(x, br, W, layer, tm):
    T = x.shape[0]
    full = lambda a: pl.BlockSpec((None,) + a.shape[1:], lambda i: (layer,) + (0,) * (a.ndim - 1))
    ws = (W["w_gate"], W["w_branch"], W["w_out"], W["ln1_g"], W["ln1_b"])
    return pl.pallas_call(
        _merge_body,
        grid=(T // tm,),
        in_specs=[pl.BlockSpec((tm, D_MODEL), lambda i: (i, 0)),
                  pl.BlockSpec((tm, N_BRANCH * BR_W), lambda i: (i, 0))] + [full(w) for w in ws],
        out_specs=pl.BlockSpec((tm, D_MODEL), lambda i: (i, 0)),
        out_shape=jax.ShapeDtypeStruct((T, D_MODEL), f32),
        compiler_params=_cparams("parallel"),
        name="merge",
    )(x, br, *ws)


ROUTE_TT = 128
_STAIR = [(k1, k2) for k1 in range(P_TOPK) for k2 in range(P_TOPK // (k1 + 1))]
N_CAND = -(-len(_STAIR) // 8) * 8


def _cand_tables():
    p12 = np.zeros((N_CAND, 2 * P_TOPK), np.float32)
    pos = np.full((N_CAND, LANE), 4.0 * P_TOPK * P_TOPK, np.float32)
    bias = np.full((N_CAND, LANE), NEG_INF, np.float32)
    for r, (k1, k2) in enumerate(_STAIR):
        p12[r, k1] = 1.0
        p12[r, P_TOPK + k2] = 1.0
        pos[r, :] = k1 * P_TOPK + k2
        bias[r, :] = 0.0
    return jnp.asarray(p12), jnp.asarray(pos), jnp.asarray(bias)


def _route_body(q_ref, keys_ref, p12_ref, cpos_ref, cbias_ref, r2_ref, e2_ref, lim_ref, cw_ref, sc_sc, v_sc):
    tt = q_ref.shape[1]
    rowf = lax.broadcasted_iota(i32, (P_NKEYS, tt), 0).astype(f32)
    k1row = lax.broadcasted_iota(i32, (P_TOPK, 1), 0).astype(f32)
    cpos = cpos_ref[...]

    for g in range(2 * P_HEADS):
        sc_sc[g] = lax.dot_general(keys_ref[g], q_ref[g].astype(bf16), _NT, preferred_element_type=f32)

    def run(tie_safe):
        bad = jnp.zeros((1, tt), f32)
        A = B = C = None
        for step in range(P_HEADS + 2):
            if step < P_HEADS:
                A = dict(h=step, work=sc_sc[pl.ds(2 * step, 2)],
                         rank=jnp.full((2, P_NKEYS, tt), float(P_TOPK), f32))
            else:
                A = None
            for k in range(P_TOPK):
                if A is not None:
                    work = A["work"]
                    m = jnp.max(work, axis=1, keepdims=True)
                    if tie_safe:
                        idx = jnp.min(jnp.where(work == m, rowf[None], float(P_NKEYS)), axis=1, keepdims=True)
                        hit = rowf[None] == idx
                    else:
                        hit = work == m
                    A["rank"] = jnp.where(hit, float(k), A["rank"])
                    A["work"] = jnp.where(hit, NEG_INF, work)
                    v_sc[A["h"] % 2, 0, k:k + 1, :] = m[0]
                    v_sc[A["h"] % 2, 1, k:k + 1, :] = m[1]
                if B is not None:
                    cand = B["cand"]
                    m = jnp.max(cand, axis=0, keepdims=True)
                    ps = jnp.min(jnp.where(cand == m, cpos, 4.0 * P_TOPK * P_TOPK), axis=0, keepdims=True)
                    B["cand"] = jnp.where(cpos == ps, NEG_INF, cand)
                    B["cnt"] = B["cnt"] + jnp.where(k1row == jnp.floor(ps * (1.0 / P_TOPK)), 1.0, 0.0)
                    if k == 0:
                        B["top0"] = m
                        B["zsum"] = jnp.ones_like(m)
                    else:
                        B["zsum"] = B["zsum"] + jnp.exp(m - B["top0"])
                if C is not None:
                    C["lim"] = jnp.where(C["rank1"] == float(k), C["cnt"][k:k + 1, :], C["lim"])
            if C is not None:
                h = C["h"]
                lim_ref[h, 0] = C["lim"]
                cw_ref[h, 0] = jnp.exp(sc_sc[2 * h] - C["top1"]) / C["zsum"]
            C = None
            if B is not None:
                C = dict(h=B["h"], rank1=B["rank1"], cnt=B["cnt"], zsum=B["zsum"], top1=B["top1"],
                         lim=jnp.zeros((P_NKEYS, tt), f32))
            B = None
            if A is not None:
                h = A["h"]
                if not tie_safe:
                    removed = jnp.sum(jnp.where(A["rank"] < float(P_TOPK), 1.0, 0.0), axis=1)
                    bad = jnp.maximum(bad, jnp.max(jnp.where(removed == float(P_TOPK), 0.0, 1.0), axis=0, keepdims=True))
                v1 = v_sc[h % 2, 0]
                v2 = v_sc[h % 2, 1]
                r2_ref[h] = A["rank"][1].astype(bf16)
                e2_ref[h] = jnp.exp(sc_sc[2 * h + 1] - v2[0:1, :]).astype(bf16)
                cand = (jnp.dot(p12_ref[:, 0:P_TOPK], v1, precision=_HI, preferred_element_type=f32)
                        + jnp.dot(p12_ref[:, P_TOPK:2 * P_TOPK], v2, precision=_HI, preferred_element_type=f32)
                        + cbias_ref[...])
                B = dict(h=h, cand=cand, cnt=jnp.zeros((P_TOPK, tt), f32), rank1=A["rank"][0], top1=v1[0:1, :])
        return bad

    bad = run(False)

    @pl.when(jnp.max(bad) > 0.0)
    def _exact_ties():
        run(True)


def _route(q16, keys):
    G, T, _ = q16.shape
    tt = ROUTE_TT
    tabs = _cand_tables()
    spec = pl.BlockSpec((P_HEADS, P_NKEYS, tt), lambda i: (0, 0, i))
    full = lambda a: pl.BlockSpec(a.shape, lambda i: (0,) * a.ndim)
    return pl.pallas_call(
        _route_body,
        grid=(T // tt,),
        in_specs=[pl.BlockSpec((G, tt, P_HALF), lambda i: (0, i, 0)), full(keys)] + [full(t) for t in tabs],
        out_specs=[spec] * 2 + [pl.BlockSpec((P_HEADS, 1, P_NKEYS, tt), lambda i: (0, i, 0, 0))] * 2,
        out_shape=[jax.ShapeDtypeStruct((P_HEADS, P_NKEYS, T), bf16)] * 2
                  + [jax.ShapeDtypeStruct((P_HEADS, T // tt, P_NKEYS, tt), f32)] * 2,
        scratch_shapes=[pltpu.VMEM((2 * P_HEADS, P_NKEYS, tt), f32), pltpu.VMEM((2, 2, P_TOPK, tt), f32)],
        compiler_params=_cparams("parallel"),
        name="peer_route",
    )(q16, keys, *tabs)


PEER_NA = 4
PEER_EB = PEER_NA * P_NKEYS
BF16_ROWS = 16
F32_ROWS = 8


def _peer_weights(act, row0, r2_ref, e2_ref, lim_ref, cw_ref, w_sc):
    tt = act.shape[1]
    grp = (P_NKEYS // BF16_ROWS, BF16_ROWS, tt)
    zero = jnp.zeros((), bf16)
    for al in range(PEER_NA):
        gate = None
        for h in range(P_HEADS):
            r = row0 + al
            rep = lambda ref, j: jnp.broadcast_to(ref[h, j, r:r + 1, :], (F32_ROWS, LANE))
            lim = jnp.concatenate([rep(lim_ref, j) for j in range(tt // LANE)], axis=1)
            cw = jnp.concatenate([rep(cw_ref, j) for j in range(tt // LANE)], axis=1)
            lim = jnp.concatenate([lim, lim], axis=0).astype(bf16)
            cw = jnp.concatenate([cw, cw], axis=0).astype(bf16)
            t = jnp.where(r2_ref[h].reshape(grp) < lim[None], e2_ref[h].reshape(grp), zero) * cw[None]
            gate = t if gate is None else gate + t
        a = act[al * P_NKEYS:(al + 1) * P_NKEYS, :].astype(bf16)
        gelu = 0.5 * a * (1.0 + lax.erf(a * (2.0 ** -0.5)))
        w_sc[(row0 + al) * P_NKEYS:(row0 + al + 1) * P_NKEYS, :] = (gelu.reshape(grp) * gate).reshape(P_NKEYS, tt)


PEER_BPS = 4


def _peer_body(x_ref, u0_ref, u1_ref, u2_ref, u3_ref, un_ref, vt_ref,
               r2_ref, e2_ref, lim_ref, cw_ref, g_ref, b_ref, o_ref, xb_sc, acc_sc, actx_sc, acty_sc, w_sc):
    s = pl.program_id(1)
    route = (r2_ref, e2_ref, lim_ref, cw_ref)

    @pl.when(s == 0)
    def _init():
        xb0 = x_ref[...].astype(bf16)
        xb_sc[...] = xb0
        acc_sc[...] = jnp.zeros_like(acc_sc)
        actx_sc[...] = lax.dot_general(u0_ref[...], xb0, _NT, preferred_element_type=f32)

    xb = xb_sc[...]
    first = lambda u_ref: lax.dot_general(u_ref[...], xb, _NT, preferred_element_type=f32)
    act1 = first(u1_ref)
    acty_sc[...] = first(u2_ref)
    _peer_weights(actx_sc[...], 0, *route, w_sc)
    _peer_weights(act1, PEER_NA, *route, w_sc)
    act3 = first(u3_ref)
    actx_sc[...] = first(un_ref)
    _peer_weights(acty_sc[...], 2 * PEER_NA, *route, w_sc)
    _peer_weights(act3, 3 * PEER_NA, *route, w_sc)
    acc_sc[...] += jnp.dot(vt_ref[...], w_sc[...], preferred_element_type=f32)

    @pl.when(s == pl.num_programs(1) - 1)
    def _finish():
        y = acc_sc[...].T
        o_ref[...] = _layernorm(DN_ALPHA * x_ref[...] + y, g_ref[...], b_ref[...])


def _peer(x, route, W, layer, tt):
    T = x.shape[0]
    r2, e2, lim, cw = route
    nblk = P_EXPERTS // PEER_EB
    ns = nblk // PEER_BPS
    tok = pl.BlockSpec((P_HEADS, P_NKEYS, tt), lambda i, s: (0, 0, i))
    per_a = pl.BlockSpec((P_HEADS, tt // LANE, PEER_BPS * PEER_NA, LANE), lambda i, s: (0, i, s, 0))
    vec = pl.BlockSpec((None, 1, D_MODEL), lambda i, s: (layer, 0, 0))
    u_blk = lambda k: pl.BlockSpec((None, PEER_EB, D_MODEL),
                                   lambda i, s: (layer, jnp.minimum(PEER_BPS * s + k, nblk - 1), 0))
    vt_blk = pl.BlockSpec((None, D_MODEL, PEER_BPS * PEER_EB), lambda i, s: (layer, 0, s))
    return pl.pallas_call(
        _peer_body,
        grid=(T // tt, ns),
        in_specs=[pl.BlockSpec((tt, D_MODEL), lambda i, s: (i, 0))]
                 + [u_blk(k) for k in range(PEER_BPS + 1)] + [vt_blk]
                 + [tok, tok, per_a, per_a, vec, vec],
        out_specs=pl.BlockSpec((tt, D_MODEL), lambda i, s: (i, 0)),
        out_shape=jax.ShapeDtypeStruct((T, D_MODEL), f32),
        scratch_shapes=[pltpu.VMEM((tt, D_MODEL), bf16), pltpu.VMEM((D_MODEL, tt), f32),
                        pltpu.VMEM((PEER_EB, tt), f32), pltpu.VMEM((PEER_EB, tt), f32),
                        pltpu.VMEM((PEER_BPS * PEER_EB, tt), bf16)],
        compiler_params=_cparams("parallel", "arbitrary"),
        name="peer_experts",
    )(x, *([W["p_u"]] * (PEER_BPS + 1)), W["p_vt"], r2, e2, lim, cw, W["ln2_g"], W["ln2_b"])


_IN_SIZES = (512, 512, 512, 512, 4, 4, 256, 256, 512, 512, 16, 512, 768, 8, 3072)


def _prep_small(m_i_bias, m_f_bias, m_norm, g_a_up, g_a_bias, g_norm,
                s_conv_w, s_conv_b, s_dt_bias, s_A_log, s_D, s_norm):
    zl = lambda n: jnp.zeros((n,), f32)
    row = lambda a: a.astype(f32).reshape(1, -1)
    return dict(
        bias_s=row(jnp.concatenate([m_i_bias, m_f_bias, zl(G_RANK), s_dt_bias, zl(LANE - SM_DT - S_HEADS)])),
        alog=row(jnp.concatenate([zl(SM_DT), s_A_log, zl(LANE - SM_DT - S_HEADS)])),
        mnorm=row(m_norm), gnorm=row(g_norm), snorm=row(s_norm),
        drow=row(jnp.repeat(s_D, S_HD)),
        gup=jnp.zeros((LANE, G_KW), f32).at[SM_GA:SM_GA + G_RANK].set(g_a_up).astype(bf16),
        gab=row(g_a_bias),
        cw=s_conv_w.astype(f32), cb=row(s_conv_b),
    )


def _prep_big(w_in, w_branch, w_out, ln1_g, ln1_b, p_wq, p_keys, p_u, p_v, ln2_g, ln2_b):
    offs = [0]
    for s in _IN_SIZES:
        offs.append(offs[-1] + s)
    colw = lambda i: w_in[:, :, offs[i]:offs[i + 1]]
    (mq, mk, mv, mo, mi, mf, gq, gk, gv, gr, ga, sz, sxbc, sdt, gate) = [colw(i) for i in range(15)]
    pad = jnp.zeros((DEPTH, D_MODEL, LANE - (SM_DT + S_HEADS)), w_in.dtype)
    w1 = jnp.concatenate([mq, mk, mv, mo, gq, gk, gv, gr, sz, sxbc, mi, mf, ga, sdt, pad], axis=2)
    rows = lambda a: a.astype(f32).reshape(DEPTH, 1, -1)
    return dict(
        w1=w1.astype(bf16), w_gate=gate.astype(bf16),
        w_branch=w_branch.astype(bf16), w_out=w_out.astype(bf16),
        ln1_g=rows(ln1_g), ln1_b=rows(ln1_b),
        p_wq=p_wq.astype(bf16),
        p_keys=p_keys.astype(bf16).reshape(DEPTH, 2 * P_HEADS, P_NKEYS, P_HALF),
        p_u=p_u.astype(bf16), p_vt=jnp.swapaxes(p_v.astype(bf16), 1, 2),
        ln2_g=rows(ln2_g), ln2_b=rows(ln2_b),
    )


def _run_layer(x, states, slab, W, Ws, layer, c, valid, nb, tm, peer_tt):
    B, L, _ = x.shape
    nchunks = L // valid
    Lp = nchunks * c
    if c != valid:
        xp = jnp.pad(x.reshape(B, nchunks, valid, D_MODEL), ((0, 0), (0, 0), (0, c - valid), (0, 0)))
        xp = xp.reshape(B * Lp, D_MODEL)
    else:
        xp = x.reshape(B * L, D_MODEL)
    z = _matmul(xp, W["w1"], layer, tm).reshape(B, Lp, Z_W)
    br, new_states = _mixer(z, states, slab, Ws, c, valid, nb)
    if c != valid:
        br = br.reshape(B, nchunks, c, N_BRANCH * BR_W)[:, :, :valid]
    xt = x.reshape(B * L, D_MODEL)
    h = _merge(xt, br.reshape(B * L, N_BRANCH * BR_W), W, layer, tm)
    q16 = _wq_matmul(h, W["p_wq"], layer, tm)
    route = _route(q16, W["p_keys"][layer])
    y = _peer(h, route, W, layer, peer_tt)
    return y.reshape(B, L, D_MODEL), new_states


def _zero_states(b):
    return (jnp.zeros((1, b, M_HEADS, M_HD, M_HD), f32), jnp.zeros((1, b, M_HEADS, M_HD), f32),
            jnp.zeros((1, b, M_HEADS), f32), jnp.zeros((1, b, G_HEADS, G_DK, G_DV), f32),
            jnp.zeros((1, b, S_HEADS, S_HD, S_STATE), f32), jnp.zeros((1, b, S_CONV - 1, S_XBC), f32))


PROMPT_CHUNK = 128
SAMPLE_CHUNK = 16
PROMPT_NB = 2
SAMPLE_NB = 8


def kernel(x_prompt, x_sample, state_mlstm_C, state_mlstm_n, state_mlstm_m, state_gla_S, state_ssm_h, state_conv, w_in, m_i_bias, m_f_bias, m_norm, g_a_up, g_a_bias, g_norm, s_conv_w, s_conv_b, s_dt_bias, s_A_log, s_D, s_norm, w_branch, w_out, ln1_g, ln1_b, p_wq, p_keys, p_u, p_v, ln2_g, ln2_b):
    small = (m_i_bias, m_f_bias, m_norm, g_a_up, g_a_bias, g_norm,
             s_conv_w, s_conv_b, s_dt_bias, s_A_log, s_D, s_norm)
    W = _prep_big(w_in, w_branch, w_out, ln1_g, ln1_b, p_wq, p_keys, p_u, p_v, ln2_g, ln2_b)
    in_states = (state_mlstm_C, state_mlstm_n, state_mlstm_m, state_gla_S, state_ssm_h, state_conv)
    hp, hs = x_prompt, x_sample
    ls = x_sample.shape[1]
    new_p = [[] for _ in range(6)]
    new_s = [[] for _ in range(6)]
    for l in range(DEPTH):
        Ws = _prep_small(*[w[l] for w in small])
        hp, sp = _run_layer(hp, _zero_states(hp.shape[0]), 0, W, Ws, l, PROMPT_CHUNK, PROMPT_CHUNK, PROMPT_NB, 256, 512)
        hs, ss = _run_layer(hs, in_states, l, W, Ws, l, SAMPLE_CHUNK, ls, SAMPLE_NB, 256, 512)
        for j in range(6):
            new_p[j].append(sp[j])
            new_s[j].append(ss[j])
    P = [jnp.stack(a, axis=0) for a in new_p]
    S = [jnp.stack(a, axis=0) for a in new_s]
    return (hp, hs, P[0], P[1], P[2], P[3], P[4], P[5], S[0], S[1], S[2], S[3], S[4], S[5])
```

```python
import functools
import math

import numpy as np
import jax
import jax.numpy as jnp
from jax import lax
from jax.experimental import pallas as pl
from jax.experimental.pallas import tpu as pltpu

f32, bf16, i32 = jnp.float32, jnp.bfloat16, jnp.int32
NEG_INF = float("-inf")

D_MODEL = 1024
DEPTH = 2
BR_W = 512
N_BRANCH = 3
M_HEADS, M_HD = 4, 128
G_HEADS, G_DK, G_DV = 4, 64, 128
G_KW, G_VW = G_HEADS * G_DK, G_HEADS * G_DV
G_RANK = 16
G_TAU = 16.0
S_HD, S_HEADS, S_GROUPS, S_STATE, S_CONV = 64, 8, 2, 64, 4
S_HG = S_HEADS // S_GROUPS
S_W = S_HEADS * S_HD
S_XBC = S_W + 2 * S_GROUPS * S_STATE
P_HEADS, P_NKEYS, P_HALF, P_TOPK = 8, 128, 128, 16
P_EXPERTS = P_NKEYS * P_NKEYS
DN_ALPHA = (2.0 * DEPTH) ** 0.25
EPS = 1e-5

LANE = 128

Z_MQ, Z_MK, Z_MV, Z_MO = 0, 512, 1024, 1536
Z_GQ, Z_GK, Z_GV, Z_GR = 2048, 2304, 2560, 3072
Z_SZ, Z_XBC, Z_SMALL = 3584, 4096, 4864
Z_W = Z_SMALL + LANE
SM_MI, SM_MF, SM_GA, SM_DT = 0, 4, 8, 24

VMEM_LIMIT = 56 * 1024 * 1024

_NT = (((1,), (1,)), ((), ()))
_TN = (((0,), (0,)), ((), ()))
_HI = lax.Precision.HIGHEST


def _cparams(*sem):
    return pltpu.CompilerParams(dimension_semantics=sem, vmem_limit_bytes=VMEM_LIMIT)


def _mm_body(x_ref, w_ref, o_ref):
    o_ref[...] = jnp.dot(x_ref[...].astype(bf16), w_ref[...], preferred_element_type=f32)


def _matmul(x, w, layer, tm):
    T, K = x.shape
    N = w.shape[2]
    return pl.pallas_call(
        _mm_body,
        grid=(T // tm,),
        in_specs=[pl.BlockSpec((tm, K), lambda i: (i, 0)), pl.BlockSpec((None, K, N), lambda i: (layer, 0, 0))],
        out_specs=pl.BlockSpec((tm, N), lambda i: (i, 0)),
        out_shape=jax.ShapeDtypeStruct((T, N), f32),
        compiler_params=_cparams("parallel"),
        name="in_proj",
    )(x, w)


def _wq_body(x_ref, w_ref, o_ref):
    r = jnp.dot(x_ref[...].astype(bf16), w_ref[...], preferred_element_type=f32)
    for g in range(2 * P_HEADS):
        o_ref[g] = r[:, g * P_HALF:(g + 1) * P_HALF]


def _wq_matmul(x, w, layer, tm):
    T, K = x.shape
    G = 2 * P_HEADS
    return pl.pallas_call(
        _wq_body,
        grid=(T // tm,),
        in_specs=[pl.BlockSpec((tm, K), lambda i: (i, 0)),
                  pl.BlockSpec((None, K, G * P_HALF), lambda i: (layer, 0, 0))],
        out_specs=pl.BlockSpec((G, tm, P_HALF), lambda i: (0, i, 0)),
        out_shape=jax.ShapeDtypeStruct((G, T, P_HALF), f32),
        compiler_params=_cparams("parallel"),
        name="peer_query",
    )(x, w)


def _mixer_body(c, valid, nb,
                z_ref, C0, n0, m0, S0, h0, buf0,
                bias_s, alog, mnorm, gnorm, snorm, drow, gup, gab, cw, cb,
                br_ref, C1, n1, m1, S1, h1, buf1,
                C_sc, n_sc, m_sc, S_sc, h_sc, xp_sc):
    j = pl.program_id(1)

    @pl.when(j == 0)
    def _load_state():
        C_sc[...] = C0[...]
        n_sc[...] = n0[...]
        for b in range(nb):
            for h in range(M_HEADS):
                m_sc[b, h:h + 1, :] = jnp.broadcast_to(m0[b, :, h:h + 1], (1, LANE))
        S_sc[...] = S0[...]
        h_sc[...] = h0[...]
        xp_sc[:, 5:8, :] = buf0[...]

    row = lax.broadcasted_iota(i32, (c, c), 0)
    col = lax.broadcasted_iota(i32, (c, c), 1)
    causal = row >= col
    tri = causal.astype(f32)
    lane = lax.broadcasted_iota(i32, (1, LANE), 1)
    tcol = lax.broadcasted_iota(i32, (c, 1), 0)
    tvalid = tcol < valid
    is_lf = (lane >> 2) == (SM_MF >> 2)
    is_dt = (lane >> 3) == (SM_DT >> 3)
    a_row = jnp.where(is_dt, -jnp.exp(alog[...]), 0.0)
    eye_f = (lax.broadcasted_iota(i32, (LANE, LANE), 0) == lax.broadcasted_iota(i32, (LANE, LANE), 1)).astype(f32)
    eye_b = eye_f.astype(bf16)
    tvalid_r = lax.broadcasted_iota(i32, (1, c), 1) < valid

    def xpose_b(x):
        n = x.shape[1]
        return lax.dot_general(eye_b[:n, :n], x, _NT, preferred_element_type=f32).astype(bf16)

    def xpose_f(x):
        return lax.dot_general(eye_f, x, _NT, precision=_HI, preferred_element_type=f32)


    def prologue(b):
        zs = z_ref[b, :, Z_SMALL:Z_SMALL + LANE]
        small = zs + bias_s[...]
        LI = jnp.where(tvalid, small, NEG_INF)
        LF = jnp.where(tvalid, jnp.where(is_lf, jax.nn.log_sigmoid(small), 0.0), 0.0)
        DT = jnp.where(tvalid, jnp.where(is_dt, jax.nn.softplus(small), 0.0), 0.0)
        cum = jnp.dot(tri, LF + DT * a_row, precision=_HI, preferred_element_type=f32)
        smallT = xpose_f(small)
        LIT = jnp.where(tvalid_r, smallT[SM_MI:SM_MI + F32_ROWS, :], NEG_INF)
        DTT = jnp.where(tvalid_r, jax.nn.softplus(smallT[SM_DT:SM_DT + S_HEADS, :]), 0.0)
        return dict(zs=zs, LI=LI, DT=DT, cum=cum, cumT=xpose_f(cum), LIT=LIT, DTT=DTT)

    def mlstm_head(b, h, P):
        sl = slice(h * M_HD, (h + 1) * M_HD)
        q = z_ref[b, :, Z_MQ + h * M_HD:Z_MQ + (h + 1) * M_HD] * (M_HD ** -0.5)
        k = z_ref[b, :, Z_MK + h * M_HD:Z_MK + (h + 1) * M_HD]
        v = z_ref[b, :, Z_MV + h * M_HD:Z_MV + (h + 1) * M_HD]
        qb, kb, vb = q.astype(bf16), k.astype(bf16), v.astype(bf16)
        Cm_ = C_sc[b, h]
        qk = lax.dot_general(qb, kb, _NT, preferred_element_type=f32)
        qC = lax.dot_general(qb, Cm_.astype(bf16), _NT, preferred_element_type=f32)
        yield
        b_c = P["cum"][:, SM_MF + h:SM_MF + h + 1]
        b_r = P["cumT"][SM_MF + h:SM_MF + h + 1, :]
        li_c = P["LI"][:, SM_MI + h:SM_MI + h + 1]
        li_r = P["LIT"][SM_MI + h:SM_MI + h + 1, :]
        m_prev = m_sc[b, h:h + 1, 0:1]
        a = b_c + m_prev
        d = jnp.where(causal, b_c - b_r + li_r, NEG_INF)
        m_t = jnp.maximum(a, jnp.max(d, axis=1, keepdims=True))
        s = qk * jnp.exp(d - m_t)
        e_in = jnp.exp(a - m_t)
        sv = jnp.dot(s.astype(bf16), vb, preferred_element_type=f32)
        b_last = b_c[c - 1:c, :]
        g_c = b_last - b_c + li_c
        m_new = jnp.maximum(b_last + m_prev, jnp.max(g_c, axis=0, keepdims=True))
        e_c = jnp.exp(b_last + m_prev - m_new)
        wg = jnp.exp(g_c - m_new)
        upd = jnp.dot(xpose_b((wg * v).astype(bf16)), kb, preferred_element_type=f32)
        yield
        num = sv + e_in * qC
        n_row = n_sc[b, h:h + 1, :]
        den = jnp.sum(s, axis=1, keepdims=True) + e_in * jnp.sum(q * n_row, axis=1, keepdims=True)
        hh = num / jnp.maximum(jnp.abs(den), jnp.exp(-m_t))
        C_sc[b, h] = e_c * Cm_ + upd
        n_sc[b, h:h + 1, :] = e_c * n_row + jnp.sum(wg * k, axis=0, keepdims=True)
        m_sc[b, h:h + 1, :] = jnp.broadcast_to(m_new, (1, LANE))
        mu = jnp.mean(hh, axis=1, keepdims=True)
        xc = hh - mu
        hn = xc * lax.rsqrt(jnp.mean(xc * xc, axis=1, keepdims=True) + EPS)
        mo = z_ref[b, :, Z_MO + h * M_HD:Z_MO + (h + 1) * M_HD]
        br_ref[b, :, sl] = hn * mnorm[:, sl] * jax.nn.sigmoid(mo)

    def gla_seq(b, P):
        ga = jnp.dot(P["zs"].astype(bf16), gup[...], preferred_element_type=f32) + gab[...]
        yield
        log_a = jnp.where(tvalid, jax.nn.log_sigmoid(ga) * (1.0 / G_TAU), 0.0)
        lam = jnp.dot(tri, log_a, precision=_HI, preferred_element_type=f32)
        yield
        levels = []
        m_half = c // 2
        while m_half >= 1:
            sh = int(math.log2(2 * m_half))
            if 2 * m_half >= F32_ROWS:
                blk = lam.reshape(c // (2 * m_half), 2 * m_half, G_KW)[:, m_half - 1:m_half, :]
                beta = jnp.broadcast_to(blk, (c // (2 * m_half), 2 * m_half, G_KW)).reshape(c, G_KW)
            else:
                sel = (col == ((row >> sh) << sh) + (m_half - 1)).astype(f32)
                beta = jnp.dot(sel, lam, precision=_HI, preferred_element_type=f32)
            levels.append((m_half, sh, beta))
            m_half //= 2
        dec_c = [jnp.exp(xpose_f(lam[c - F32_ROWS:c, i * LANE:(i + 1) * LANE])[:, F32_ROWS - 1:F32_ROWS])
                 for i in range(G_KW // LANE)]
        gq = z_ref[b, :, Z_GQ:Z_GQ + G_KW] * (G_DK ** -0.5)
        gk = z_ref[b, :, Z_GK:Z_GK + G_KW]
        yield
        att = [None] * G_HEADS
        for m_half, sh, beta in levels:
            upper = (tcol & (2 * m_half - 1)) >= m_half
            qm = (gq * jnp.exp(jnp.where(upper, lam - beta, NEG_INF))).astype(bf16)
            km = (gk * jnp.exp(jnp.where(upper, NEG_INF, beta - lam))).astype(bf16)
            ps = [lax.dot_general(qm[:, h * G_DK:(h + 1) * G_DK], km[:, h * G_DK:(h + 1) * G_DK], _NT,
                                  preferred_element_type=f32) for h in range(G_HEADS)]
            yield
            if 2 * m_half < c:
                same_blk = (row >> sh) == (col >> sh)
                ps = [jnp.where(same_blk, p, 0.0) for p in ps]
            att = [p if a_ is None else a_ + p for a_, p in zip(att, ps)]
        q_in = (gq * jnp.exp(lam)).astype(bf16)
        k_out = (gk * jnp.exp(lam[c - 1:c, :] - lam)).astype(bf16)
        k_outT = [xpose_b(k_out[:, i * LANE:(i + 1) * LANE]) for i in range(G_KW // LANE)]
        hpl = LANE // G_DK
        for h in range(G_HEADS):
            ks = slice(h * G_DK, (h + 1) * G_DK)
            vs = slice(h * G_DV, (h + 1) * G_DV)
            v = z_ref[b, :, Z_GV + h * G_DV:Z_GV + (h + 1) * G_DV]
            vb = v.astype(bf16)
            S_ = S_sc[b, h]
            o_att = jnp.dot(att[h].astype(bf16), vb, preferred_element_type=f32)
            o_st = jnp.dot(q_in[:, ks], S_.astype(bf16), preferred_element_type=f32)
            rs = slice((h % hpl) * G_DK, (h % hpl + 1) * G_DK)
            upd = jnp.dot(k_outT[h // hpl][rs, :], vb, preferred_element_type=f32)
            yield
            diag = jnp.sum(gq[:, ks] * gk[:, ks], axis=1, keepdims=True)
            o = o_att + diag * v + o_st
            S_sc[b, h] = dec_c[h // hpl][rs, :] * S_ + upd
            on = o * lax.rsqrt(jnp.mean(o * o, axis=1, keepdims=True) + EPS)
            gr = z_ref[b, :, Z_GR + h * G_DV:Z_GR + (h + 1) * G_DV]
            br_ref[b, :, BR_W + h * G_DV:BR_W + (h + 1) * G_DV] = on * gnorm[:, vs] * jax.nn.silu(gr)

    def ssd_group(b, g, P, act):
        Bm = act[:, S_W + g * S_STATE:S_W + (g + 1) * S_STATE].astype(bf16)
        c0 = S_W + S_GROUPS * S_STATE + g * S_STATE
        Cmat = act[:, c0:c0 + S_STATE].astype(bf16)
        cbm = lax.dot_general(Cmat, Bm, _NT, preferred_element_type=f32)
        yield
        ys = []
        ssq = None
        for hg in range(S_HG):
            hh_ = g * S_HG + hg
            ps = slice(hh_ * S_HD, (hh_ + 1) * S_HD)
            xs = act[:, ps]
            dt_c = P["DT"][:, SM_DT + hh_:SM_DT + hh_ + 1]
            dt_r = P["DTT"][hh_:hh_ + 1, :]
            l_c = P["cum"][:, SM_DT + hh_:SM_DT + hh_ + 1]
            l_r = P["cumT"][SM_DT + hh_:SM_DT + hh_ + 1, :]
            decay = jnp.exp(jnp.where(causal, l_c - l_r, NEG_INF))
            w = (cbm * decay * dt_r).astype(bf16)
            hs_ = h_sc[b, hh_]
            l_last = l_c[c - 1:c, :]
            ws = jnp.exp(l_last - l_c) * dt_c
            y_in = jnp.dot(w, xs.astype(bf16), preferred_element_type=f32)
            y_st = lax.dot_general(Cmat, hs_.astype(bf16), _NT, preferred_element_type=f32)
            upd = jnp.dot(xpose_b((ws * xs).astype(bf16)), Bm, preferred_element_type=f32)
            yield
            y = y_in + y_st * jnp.exp(l_c)
            h_sc[b, hh_] = jnp.exp(l_last) * hs_ + upd
            yy = (y + drow[:, ps] * xs) * jax.nn.silu(z_ref[b, :, Z_SZ + hh_ * S_HD:Z_SZ + (hh_ + 1) * S_HD])
            ys.append(yy)
            sq = jnp.sum(yy * yy, axis=1, keepdims=True)
            ssq = sq if ssq is None else ssq + sq
        scale = lax.rsqrt(ssq * (1.0 / (S_W // S_GROUPS)) + EPS)
        for hg in range(S_HG):
            hh_ = g * S_HG + hg
            ps = slice(hh_ * S_HD, (hh_ + 1) * S_HD)
            br_ref[b, :, 2 * BR_W + hh_ * S_HD:2 * BR_W + (hh_ + 1) * S_HD] = ys[hg] * scale * snorm[:, ps]

    streams = []
    for b in range(nb):
        P = prologue(b)
        xp_sc[b, 8:8 + c, :] = z_ref[b, :, Z_XBC:Z_XBC + S_XBC]
        conv = cb[...] + cw[0:1, :] * xp_sc[b, pl.ds(5, c), :]
        for jj in range(1, S_CONV):
            conv = conv + cw[jj:jj + 1, :] * xp_sc[b, pl.ds(5 + jj, c), :]
        new_buf = xp_sc[b, pl.ds(5 + valid, S_CONV - 1), :]
        xp_sc[b, 5:8, :] = new_buf
        act = jax.nn.silu(conv)
        streams += [mlstm_head(b, h, P) for h in range(M_HEADS)]
        streams += [gla_seq(b, P)]
        streams += [ssd_group(b, g, P, act) for g in range(S_GROUPS)]
    while streams:
        alive = []
        for st in streams:
            try:
                next(st)
                alive.append(st)
            except StopIteration:
                pass
        streams = alive

    @pl.when(j == pl.num_programs(1) - 1)
    def _store_state():
        C1[...] = C_sc[...]
        n1[...] = n_sc[...]
        m1[...] = m_sc[...]
        S1[...] = S_sc[...]
        h1[...] = h_sc[...]
        buf1[...] = xp_sc[:, 5:8, :]


def _mixer(z, states, layer, W, c, valid, nb):
    B, L, _ = z.shape
    nc = L // c
    C0, n0, m0, S0, h0, buf0 = states
    m0 = m0.reshape(m0.shape[0], B, 1, M_HEADS)

    def per_b(shape):
        nd = len(shape)
        return pl.BlockSpec((nb,) + shape, lambda b, j: (b,) + (0,) * nd)

    def per_b_in(shape):
        nd = len(shape)
        return pl.BlockSpec((None, nb) + shape, lambda b, j: (layer, b) + (0,) * nd)

    def const(shape):
        nd = len(shape)
        return pl.BlockSpec(shape, lambda b, j: (0,) * nd)

    small_w = (W["bias_s"], W["alog"], W["mnorm"], W["gnorm"], W["snorm"], W["drow"],
               W["gup"], W["gab"], W["cw"], W["cb"])
    st_shapes = [(M_HEADS, M_HD, M_HD), (M_HEADS, M_HD), (M_HEADS, LANE),
                 (G_HEADS, G_DK, G_DV), (S_HEADS, S_HD, S_STATE), (S_CONV - 1, S_XBC)]
    out = pl.pallas_call(
        functools.partial(_mixer_body, c, valid, nb),
        grid=(B // nb, nc),
        in_specs=[pl.BlockSpec((nb, c, Z_W), lambda b, j: (b, j, 0)),
                  per_b_in((M_HEADS, M_HD, M_HD)), per_b_in((M_HEADS, M_HD)), per_b_in((1, M_HEADS)),
                  per_b_in((G_HEADS, G_DK, G_DV)), per_b_in((S_HEADS, S_HD, S_STATE)),
                  per_b_in((S_CONV - 1, S_XBC))]
                 + [const(w.shape) for w in small_w],
        out_specs=[pl.BlockSpec((nb, c, N_BRANCH * BR_W), lambda b, j: (b, j, 0))]
                  + [per_b(s) for s in st_shapes],
        out_shape=[jax.ShapeDtypeStruct((B, L, N_BRANCH * BR_W), f32)]
                  + [jax.ShapeDtypeStruct((B,) + s, f32) for s in st_shapes],
        scratch_shapes=[pltpu.VMEM((nb, M_HEADS, M_HD, M_HD), f32), pltpu.VMEM((nb, M_HEADS, M_HD), f32),
                        pltpu.VMEM((nb, M_HEADS, LANE), f32), pltpu.VMEM((nb, G_HEADS, G_DK, G_DV), f32),
                        pltpu.VMEM((nb, S_HEADS, S_HD, S_STATE), f32), pltpu.VMEM((nb, c + 8, S_XBC), f32)],
        compiler_params=_cparams("parallel", "arbitrary"),
        name="mixer",
    )(z, C0, n0, m0, S0, h0, buf0, *small_w)
    br, C1, n1, m1, S1, h1, buf1 = out
    return br, (C1, n1, m1[:, :, 0], S1, h1, buf1)


def _layernorm(x, g, b):
    mu = jnp.mean(x, axis=1, keepdims=True)
    xc = x - mu
    return xc * lax.rsqrt(jnp.mean(xc * xc, axis=1, keepdims=True) + EPS) * g + b


def _merge_body(x_ref, br_ref, wg_ref, wb_ref, wo_ref, g_ref, b_ref, o_ref):
    x = x_ref[...]
    xb = x.astype(bf16)
    mixed = None
    for n in range(N_BRANCH):
        gate = jnp.dot(xb, wg_ref[:, n * D_MODEL:(n + 1) * D_MODEL], preferred_element_type=f32)
        proj = jnp.dot(br_ref[:, n * BR_W:(n + 1) * BR_W].astype(bf16), wb_ref[n], preferred_element_type=f32)
        t = jax.nn.sigmoid(gate) * proj
        mixed = t if mixed is None else mixed + t
    y = jnp.dot(mixed.astype(bf16), wo_ref[...], preferred_element_type=f32)
    o_ref[...] = _layernorm(DN_ALPHA * x + y, g_ref[...], b_ref[...])


def _merge(x, br, W, layer, tm):
    T = x.shape[0]
    full = lambda a: pl.BlockSpec((None,) + a.shape[1:], lambda i: (layer,) + (0,) * (a.ndim - 1))
    ws = (W["w_gate"], W["w_branch"], W["w_out"], W["ln1_g"], W["ln1_b"])
    return pl.pallas_call(
        _merge_body,
        grid=(T // tm,),
        in_specs=[pl.BlockSpec((tm, D_MODEL), lambda i: (i, 0)),
                  pl.BlockSpec((tm, N_BRANCH * BR_W), lambda i: (i, 0))] + [full(w) for w in ws],
        out_specs=pl.BlockSpec((tm, D_MODEL), lambda i: (i, 0)),
        out_shape=jax.ShapeDtypeStruct((T, D_MODEL), f32),
        compiler_params=_cparams("parallel"),
        name="merge",
    )(x, br, *ws)


ROUTE_TT = 128
_STAIR = [(k1, k2) for k1 in range(P_TOPK) for k2 in range(P_TOPK // (k1 + 1))]
N_CAND = -(-len(_STAIR) // 8) * 8


def _cand_tables():
    p12 = np.zeros((N_CAND, 2 * P_TOPK), np.float32)
    pos = np.full((N_CAND, LANE), 4.0 * P_TOPK * P_TOPK, np.float32)
    bias = np.full((N_CAND, LANE), NEG_INF, np.float32)
    for r, (k1, k2) in enumerate(_STAIR):
        p12[r, k1] = 1.0
        p12[r, P_TOPK + k2] = 1.0
        pos[r, :] = k1 * P_TOPK + k2
        bias[r, :] = 0.0
    return jnp.asarray(p12), jnp.asarray(pos), jnp.asarray(bias)


def _route_body(q_ref, keys_ref, p12_ref, cpos_ref, cbias_ref, r2_ref, e2_ref, lim_ref, cw_ref, sc_sc, v_sc):
    tt = q_ref.shape[1]
    rowf = lax.broadcasted_iota(i32, (P_NKEYS, tt), 0).astype(f32)
    k1row = lax.broadcasted_iota(i32, (P_TOPK, 1), 0).astype(f32)
    cpos = cpos_ref[...]

    for g in range(2 * P_HEADS):
        sc_sc[g] = lax.dot_general(keys_ref[g], q_ref[g].astype(bf16), _NT, preferred_element_type=f32)

    def run(tie_safe):
        bad = jnp.zeros((1, tt), f32)
        A = B = C = None
        for step in range(P_HEADS + 2):
            if step < P_HEADS:
                A = dict(h=step, work=sc_sc[pl.ds(2 * step, 2)],
                         rank=jnp.full((2, P_NKEYS, tt), float(P_TOPK), f32))
            else:
                A = None
            for k in range(P_TOPK):
                if A is not None:
                    work = A["work"]
                    m = jnp.max(work, axis=1, keepdims=True)
                    if tie_safe:
                        idx = jnp.min(jnp.where(work == m, rowf[None], float(P_NKEYS)), axis=1, keepdims=True)
                        hit = rowf[None] == idx
                    else:
                        hit = work == m
                    A["rank"] = jnp.where(hit, float(k), A["rank"])
                    A["work"] = jnp.where(hit, NEG_INF, work)
                    v_sc[A["h"] % 2, 0, k:k + 1, :] = m[0]
                    v_sc[A["h"] % 2, 1, k:k + 1, :] = m[1]
                if B is not None:
                    cand = B["cand"]
                    m = jnp.max(cand, axis=0, keepdims=True)
                    ps = jnp.min(jnp.where(cand == m, cpos, 4.0 * P_TOPK * P_TOPK), axis=0, keepdims=True)
                    B["cand"] = jnp.where(cpos == ps, NEG_INF, cand)
                    B["cnt"] = B["cnt"] + jnp.where(k1row == jnp.floor(ps * (1.0 / P_TOPK)), 1.0, 0.0)
                    if k == 0:
                        B["top0"] = m
                        B["zsum"] = jnp.ones_like(m)
                    else:
                        B["zsum"] = B["zsum"] + jnp.exp(m - B["top0"])
                if C is not None:
                    C["lim"] = jnp.where(C["rank1"] == float(k), C["cnt"][k:k + 1, :], C["lim"])
            if C is not None:
                h = C["h"]
                lim_ref[h, 0] = C["lim"]
                cw_ref[h, 0] = jnp.exp(sc_sc[2 * h] - C["top1"]) / C["zsum"]
            C = None
            if B is not None:
                C = dict(h=B["h"], rank1=B["rank1"], cnt=B["cnt"], zsum=B["zsum"], top1=B["top1"],
                         lim=jnp.zeros((P_NKEYS, tt), f32))
            B = None
            if A is not None:
                h = A["h"]
                if not tie_safe:
                    removed = jnp.sum(jnp.where(A["rank"] < float(P_TOPK), 1.0, 0.0), axis=1)
                    bad = jnp.maximum(bad, jnp.max(jnp.where(removed == float(P_TOPK), 0.0, 1.0), axis=0, keepdims=True))
                v1 = v_sc[h % 2, 0]
                v2 = v_sc[h % 2, 1]
                r2_ref[h] = A["rank"][1].astype(bf16)
                e2_ref[h] = jnp.exp(sc_sc[2 * h + 1] - v2[0:1, :]).astype(bf16)
                cand = (jnp.dot(p12_ref[:, 0:P_TOPK], v1, precision=_HI, preferred_element_type=f32)
                        + jnp.dot(p12_ref[:, P_TOPK:2 * P_TOPK], v2, precision=_HI, preferred_element_type=f32)
                        + cbias_ref[...])
                B = dict(h=h, cand=cand, cnt=jnp.zeros((P_TOPK, tt), f32), rank1=A["rank"][0], top1=v1[0:1, :])
        return bad

    bad = run(False)

    @pl.when(jnp.max(bad) > 0.0)
    def _exact_ties():
        run(True)


def _route(q16, keys):
    G, T, _ = q16.shape
    tt = ROUTE_TT
    tabs = _cand_tables()
    spec = pl.BlockSpec((P_HEADS, P_NKEYS, tt), lambda i: (0, 0, i))
    full = lambda a: pl.BlockSpec(a.shape, lambda i: (0,) * a.ndim)
    return pl.pallas_call(
        _route_body,
        grid=(T // tt,),
        in_specs=[pl.BlockSpec((G, tt, P_HALF), lambda i: (0, i, 0)), full(keys)] + [full(t) for t in tabs],
        out_specs=[spec] * 2 + [pl.BlockSpec((P_HEADS, 1, P_NKEYS, tt), lambda i: (0, i, 0, 0))] * 2,
        out_shape=[jax.ShapeDtypeStruct((P_HEADS, P_NKEYS, T), bf16)] * 2
                  + [jax.ShapeDtypeStruct((P_HEADS, T // tt, P_NKEYS, tt), f32)] * 2,
        scratch_shapes=[pltpu.VMEM((2 * P_HEADS, P_NKEYS, tt), f32), pltpu.VMEM((2, 2, P_TOPK, tt), f32)],
        compiler_params=_cparams("parallel"),
        name="peer_route",
    )(q16, keys, *tabs)


PEER_NA = 4
PEER_EB = PEER_NA * P_NKEYS
BF16_ROWS = 16
F32_ROWS = 8


def _peer_weights(act, row0, r2_ref, e2_ref, lim_ref, cw_ref, w_sc):
    tt = act.shape[1]
    grp = (P_NKEYS // BF16_ROWS, BF16_ROWS, tt)
    zero = jnp.zeros((), bf16)
    for al in range(PEER_NA):
        gate = None
        for h in range(P_HEADS):
            r = row0 + al
            rep = lambda ref, j: jnp.broadcast_to(ref[h, j, r:r + 1, :], (F32_ROWS, LANE))
            lim = jnp.concatenate([rep(lim_ref, j) for j in range(tt // LANE)], axis=1)
            cw = jnp.concatenate([rep(cw_ref, j) for j in range(tt // LANE)], axis=1)
            lim = jnp.concatenate([lim, lim], axis=0).astype(bf16)
            cw = jnp.concatenate([cw, cw], axis=0).astype(bf16)
            t = jnp.where(r2_ref[h].reshape(grp) < lim[None], e2_ref[h].reshape(grp), zero) * cw[None]
            gate = t if gate is None else gate + t
        a = act[al * P_NKEYS:(al + 1) * P_NKEYS, :].astype(bf16)
        gelu = 0.5 * a * (1.0 + lax.erf(a * (2.0 ** -0.5)))
        w_sc[(row0 + al) * P_NKEYS:(row0 + al + 1) * P_NKEYS, :] = (gelu.reshape(grp) * gate).reshape(P_NKEYS, tt)


PEER_BPS = 4


def _peer_body(x_ref, u0_ref, u1_ref, u2_ref, u3_ref, un_ref, vt_ref,
               r2_ref, e2_ref, lim_ref, cw_ref, g_ref, b_ref, o_ref, xb_sc, acc_sc, actx_sc, acty_sc, w_sc):
    s = pl.program_id(1)
    route = (r2_ref, e2_ref, lim_ref, cw_ref)

    @pl.when(s == 0)
    def _init():
        xb0 = x_ref[...].astype(bf16)
        xb_sc[...] = xb0
        acc_sc[...] = jnp.zeros_like(acc_sc)
        actx_sc[...] = lax.dot_general(u0_ref[...], xb0, _NT, preferred_element_type=f32)

    xb = xb_sc[...]
    first = lambda u_ref: lax.dot_general(u_ref[...], xb, _NT, preferred_element_type=f32)
    act1 = first(u1_ref)
    acty_sc[...] = first(u2_ref)
    _peer_weights(actx_sc[...], 0, *route, w_sc)
    _peer_weights(act1, PEER_NA, *route, w_sc)
    act3 = first(u3_ref)
    actx_sc[...] = first(un_ref)
    _peer_weights(acty_sc[...], 2 * PEER_NA, *route, w_sc)
    _peer_weights(act3, 3 * PEER_NA, *route, w_sc)
    acc_sc[...] += jnp.dot(vt_ref[...], w_sc[...], preferred_element_type=f32)

    @pl.when(s == pl.num_programs(1) - 1)
    def _finish():
        y = acc_sc[...].T
        o_ref[...] = _layernorm(DN_ALPHA * x_ref[...] + y, g_ref[...], b_ref[...])


def _peer(x, route, W, layer, tt):
    T = x.shape[0]
    r2, e2, lim, cw = route
    nblk = P_EXPERTS // PEER_EB
    ns = nblk // PEER_BPS
    tok = pl.BlockSpec((P_HEADS, P_NKEYS, tt), lambda i, s: (0, 0, i))
    per_a = pl.BlockSpec((P_HEADS, tt // LANE, PEER_BPS * PEER_NA, LANE), lambda i, s: (0, i, s, 0))
    vec = pl.BlockSpec((None, 1, D_MODEL), lambda i, s: (layer, 0, 0))
    u_blk = lambda k: pl.BlockSpec((None, PEER_EB, D_MODEL),
                                   lambda i, s: (layer, jnp.minimum(PEER_BPS * s + k, nblk - 1), 0))
    vt_blk = pl.BlockSpec((None, D_MODEL, PEER_BPS * PEER_EB), lambda i, s: (layer, 0, s))
    return pl.pallas_call(
        _peer_body,
        grid=(T // tt, ns),
        in_specs=[pl.BlockSpec((tt, D_MODEL), lambda i, s: (i, 0))]
                 + [u_blk(k) for k in range(PEER_BPS + 1)] + [vt_blk]
                 + [tok, tok, per_a, per_a, vec, vec],
        out_specs=pl.BlockSpec((tt, D_MODEL), lambda i, s: (i, 0)),
        out_shape=jax.ShapeDtypeStruct((T, D_MODEL), f32),
        scratch_shapes=[pltpu.VMEM((tt, D_MODEL), bf16), pltpu.VMEM((D_MODEL, tt), f32),
                        pltpu.VMEM((PEER_EB, tt), f32), pltpu.VMEM((PEER_EB, tt), f32),
                        pltpu.VMEM((PEER_BPS * PEER_EB, tt), bf16)],
        compiler_params=_cparams("parallel", "arbitrary"),
        name="peer_experts",
    )(x, *([W["p_u"]] * (PEER_BPS + 1)), W["p_vt"], r2, e2, lim, cw, W["ln2_g"], W["ln2_b"])


_IN_SIZES = (512, 512, 512, 512, 4, 4, 256, 256, 512, 512, 16, 512, 768, 8, 3072)


def _prep_small(m_i_bias, m_f_bias, m_norm, g_a_up, g_a_bias, g_norm,
                s_conv_w, s_conv_b, s_dt_bias, s_A_log, s_D, s_norm):
    zl = lambda n: jnp.zeros((n,), f32)
    row = lambda a: a.astype(f32).reshape(1, -1)
    return dict(
        bias_s=row(jnp.concatenate([m_i_bias, m_f_bias, zl(G_RANK), s_dt_bias, zl(LANE - SM_DT - S_HEADS)])),
        alog=row(jnp.concatenate([zl(SM_DT), s_A_log, zl(LANE - SM_DT - S_HEADS)])),
        mnorm=row(m_norm), gnorm=row(g_norm), snorm=row(s_norm),
        drow=row(jnp.repeat(s_D, S_HD)),
        gup=jnp.zeros((LANE, G_KW), f32).at[SM_GA:SM_GA + G_RANK].set(g_a_up).astype(bf16),
        gab=row(g_a_bias),
        cw=s_conv_w.astype(f32), cb=row(s_conv_b),
    )


def _prep_big(w_in, w_branch, w_out, ln1_g, ln1_b, p_wq, p_keys, p_u, p_v, ln2_g, ln2_b):
    offs = [0]
    for s in _IN_SIZES:
        offs.append(offs[-1] + s)
    colw = lambda i: w_in[:, :, offs[i]:offs[i + 1]]
    (mq, mk, mv, mo, mi, mf, gq, gk, gv, gr, ga, sz, sxbc, sdt, gate) = [colw(i) for i in range(15)]
    pad = jnp.zeros((DEPTH, D_MODEL, LANE - (SM_DT + S_HEADS)), w_in.dtype)
    w1 = jnp.concatenate([mq, mk, mv, mo, gq, gk, gv, gr, sz, sxbc, mi, mf, ga, sdt, pad], axis=2)
    rows = lambda a: a.astype(f32).reshape(DEPTH, 1, -1)
    return dict(
        w1=w1.astype(bf16), w_gate=gate.astype(bf16),
        w_branch=w_branch.astype(bf16), w_out=w_out.astype(bf16),
        ln1_g=rows(ln1_g), ln1_b=rows(ln1_b),
        p_wq=p_wq.astype(bf16),
        p_keys=p_keys.astype(bf16).reshape(DEPTH, 2 * P_HEADS, P_NKEYS, P_HALF),
        p_u=p_u.astype(bf16), p_vt=jnp.swapaxes(p_v.astype(bf16), 1, 2),
        ln2_g=rows(ln2_g), ln2_b=rows(ln2_b),
    )


def _run_layer(x, states, slab, W, Ws, layer, c, valid, nb, tm, peer_tt):
    B, L, _ = x.shape
    nchunks = L // valid
    Lp = nchunks * c
    if c != valid:
        xp = jnp.pad(x.reshape(B, nchunks, valid, D_MODEL), ((0, 0), (0, 0), (0, c - valid), (0, 0)))
        xp = xp.reshape(B * Lp, D_MODEL)
    else:
        xp = x.reshape(B * L, D_MODEL)
    z = _matmul(xp, W["w1"], layer, tm).reshape(B, Lp, Z_W)
    br, new_states = _mixer(z, states, slab, Ws, c, valid, nb)
    if c != valid:
        br = br.reshape(B, nchunks, c, N_BRANCH * BR_W)[:, :, :valid]
    xt = x.reshape(B * L, D_MODEL)
    h = _merge(xt, br.reshape(B * L, N_BRANCH * BR_W), W, layer, tm)
    q16 = _wq_matmul(h, W["p_wq"], layer, tm)
    route = _route(q16, W["p_keys"][layer])
    y = _peer(h, route, W, layer, peer_tt)
    return y.reshape(B, L, D_MODEL), new_states


def _zero_states(b):
    return (jnp.zeros((1, b, M_HEADS, M_HD, M_HD), f32), jnp.zeros((1, b, M_HEADS, M_HD), f32),
            jnp.zeros((1, b, M_HEADS), f32), jnp.zeros((1, b, G_HEADS, G_DK, G_DV), f32),
            jnp.zeros((1, b, S_HEADS, S_HD, S_STATE), f32), jnp.zeros((1, b, S_CONV - 1, S_XBC), f32))


PROMPT_CHUNK = 128
SAMPLE_CHUNK = 16
PROMPT_NB = 2
SAMPLE_NB = 8
PROJ_TM = 512
PEER_TT = 512


def kernel(x_prompt, x_sample, state_mlstm_C, state_mlstm_n, state_mlstm_m, state_gla_S, state_ssm_h, state_conv, w_in, m_i_bias, m_f_bias, m_norm, g_a_up, g_a_bias, g_norm, s_conv_w, s_conv_b, s_dt_bias, s_A_log, s_D, s_norm, w_branch, w_out, ln1_g, ln1_b, p_wq, p_keys, p_u, p_v, ln2_g, ln2_b):
    small = (m_i_bias, m_f_bias, m_norm, g_a_up, g_a_bias, g_norm,
             s_conv_w, s_conv_b, s_dt_bias, s_A_log, s_D, s_norm)
    W = _prep_big(w_in, w_branch, w_out, ln1_g, ln1_b, p_wq, p_keys, p_u, p_v, ln2_g, ln2_b)
    in_states = (state_mlstm_C, state_mlstm_n, state_mlstm_m, state_gla_S, state_ssm_h, state_conv)
    hp, hs = x_prompt, x_sample
    ls = x_sample.shape[1]
    new_p = [[] for _ in range(6)]
    new_s = [[] for _ in range(6)]
    for l in range(DEPTH):
        Ws = _prep_small(*[w[l] for w in small])
        hp, sp = _run_layer(hp, _zero_states(hp.shape[0]), 0, W, Ws, l, PROMPT_CHUNK, PROMPT_CHUNK, PROMPT_NB,
                            PROJ_TM, PEER_TT)
        hs, ss = _run_layer(hs, in_states, l, W, Ws, l, SAMPLE_CHUNK, ls, SAMPLE_NB, PROJ_TM, PEER_TT)
        for j in range(6):
            new_p[j].append(sp[j])
            new_s[j].append(ss[j])
    P = [jnp.stack(a, axis=0) for a in new_p]
    S = [jnp.stack(a, axis=0) for a in new_s]
    return (hp, hs, P[0], P[1], P[2], P[3], P[4], P[5], S[0], S[1], S[2], S[3], S[4], S[5])
```

```python
import functools
import math

import numpy as np
import jax
import jax.numpy as jnp
from jax import lax
from jax.experimental import pallas as pl
from jax.experimental.pallas import tpu as pltpu

f32, bf16, i32 = jnp.float32, jnp.bfloat16, jnp.int32
NEG_INF = float("-inf")

D_MODEL = 1024
DEPTH = 2
BR_W = 512
N_BRANCH = 3
M_HEADS, M_HD = 4, 128
G_HEADS, G_DK, G_DV = 4, 64, 128
G_KW, G_VW = G_HEADS * G_DK, G_HEADS * G_DV
G_RANK = 16
G_TAU = 16.0
S_HD, S_HEADS, S_GROUPS, S_STATE, S_CONV = 64, 8, 2, 64, 4
S_HG = S_HEADS // S_GROUPS
S_W = S_HEADS * S_HD
S_XBC = S_W + 2 * S_GROUPS * S_STATE
P_HEADS, P_NKEYS, P_HALF, P_TOPK = 8, 128, 128, 16
P_EXPERTS = P_NKEYS * P_NKEYS
DN_ALPHA = (2.0 * DEPTH) ** 0.25
EPS = 1e-5

LANE = 128

Z_MQ, Z_MK, Z_MV, Z_MO = 0, 512, 1024, 1536
Z_GQ, Z_GK, Z_GV, Z_GR = 2048, 2304, 2560, 3072
Z_SZ, Z_XBC, Z_SMALL = 3584, 4096, 4864
Z_W = Z_SMALL + LANE
SM_MI, SM_MF, SM_GA, SM_DT = 0, 4, 8, 24

VMEM_LIMIT = 56 * 1024 * 1024

_NT = (((1,), (1,)), ((), ()))
_TN = (((0,), (0,)), ((), ()))
_HI = lax.Precision.HIGHEST


def _cparams(*sem):
    return pltpu.CompilerParams(dimension_semantics=sem, vmem_limit_bytes=VMEM_LIMIT)


def _mm_body(x_ref, w_ref, o_ref):
    o_ref[...] = jnp.dot(x_ref[...].astype(bf16), w_ref[...], preferred_element_type=f32)


def _matmul(x, w, layer, tm):
    T, K = x.shape
    N = w.shape[2]
    return pl.pallas_call(
        _mm_body,
        grid=(T // tm,),
        in_specs=[pl.BlockSpec((tm, K), lambda i: (i, 0)), pl.BlockSpec((None, K, N), lambda i: (layer, 0, 0))],
        out_specs=pl.BlockSpec((tm, N), lambda i: (i, 0)),
        out_shape=jax.ShapeDtypeStruct((T, N), f32),
        compiler_params=_cparams("parallel"),
        name="in_proj",
    )(x, w)


def _wq_body(x_ref, w_ref, o_ref):
    r = jnp.dot(x_ref[...].astype(bf16), w_ref[...], preferred_element_type=f32)
    for g in range(2 * P_HEADS):
        o_ref[g] = r[:, g * P_HALF:(g + 1) * P_HALF]


def _wq_matmul(x, w, layer, tm):
    T, K = x.shape
    G = 2 * P_HEADS
    return pl.pallas_call(
        _wq_body,
        grid=(T // tm,),
        in_specs=[pl.BlockSpec((tm, K), lambda i: (i, 0)),
                  pl.BlockSpec((None, K, G * P_HALF), lambda i: (layer, 0, 0))],
        out_specs=pl.BlockSpec((G, tm, P_HALF), lambda i: (0, i, 0)),
        out_shape=jax.ShapeDtypeStruct((G, T, P_HALF), f32),
        compiler_params=_cparams("parallel"),
        name="peer_query",
    )(x, w)


def _mixer_body(c, valid, nb,
                z_ref, C0, n0, m0, S0, h0, buf0,
                bias_s, alog, mnorm, gnorm, snorm, drow, gup, gab, cw, cb,
                br_ref, C1, n1, m1, S1, h1, buf1,
                C_sc, n_sc, m_sc, S_sc, h_sc, xp_sc):
    j = pl.program_id(1)

    @pl.when(j == 0)
    def _load_state():
        C_sc[...] = C0[...]
        n_sc[...] = n0[...]
        for b in range(nb):
            for h in range(M_HEADS):
                m_sc[b, h:h + 1, :] = jnp.broadcast_to(m0[b, :, h:h + 1], (1, LANE))
        S_sc[...] = S0[...]
        h_sc[...] = h0[...]
        xp_sc[:, 5:8, :] = buf0[...]

    row = lax.broadcasted_iota(i32, (c, c), 0)
    col = lax.broadcasted_iota(i32, (c, c), 1)
    causal = row >= col
    tri = causal.astype(f32)
    lane = lax.broadcasted_iota(i32, (1, LANE), 1)
    tcol = lax.broadcasted_iota(i32, (c, 1), 0)
    tvalid = tcol < valid
    is_lf = (lane >> 2) == (SM_MF >> 2)
    is_dt = (lane >> 3) == (SM_DT >> 3)
    a_row = jnp.where(is_dt, -jnp.exp(alog[...]), 0.0)
    eye_f = (lax.broadcasted_iota(i32, (LANE, LANE), 0) == lax.broadcasted_iota(i32, (LANE, LANE), 1)).astype(f32)
    eye_b = eye_f.astype(bf16)
    tvalid_r = lax.broadcasted_iota(i32, (1, c), 1) < valid

    def xpose_b(x):
        n = x.shape[1]
        return lax.dot_general(eye_b[:n, :n], x, _NT, preferred_element_type=f32).astype(bf16)

    def xpose_f(x):
        return lax.dot_general(eye_f, x, _NT, precision=_HI, preferred_element_type=f32)


    def prologue(b):
        zs = z_ref[b, :, Z_SMALL:Z_SMALL + LANE]
        small = zs + bias_s[...]
        LI = jnp.where(tvalid, small, NEG_INF)
        LF = jnp.where(tvalid, jnp.where(is_lf, jax.nn.log_sigmoid(small), 0.0), 0.0)
        DT = jnp.where(tvalid, jnp.where(is_dt, jax.nn.softplus(small), 0.0), 0.0)
        cum = jnp.dot(tri, LF + DT * a_row, precision=_HI, preferred_element_type=f32)
        smallT = xpose_f(small)
        LIT = jnp.where(tvalid_r, smallT[SM_MI:SM_MI + F32_ROWS, :], NEG_INF)
        DTT = jnp.where(tvalid_r, jax.nn.softplus(smallT[SM_DT:SM_DT + S_HEADS, :]), 0.0)
        return dict(zs=zs, LI=LI, DT=DT, cum=cum, cumT=xpose_f(cum), LIT=LIT, DTT=DTT)

    def mlstm_head(b, h, P):
        sl = slice(h * M_HD, (h + 1) * M_HD)
        q = z_ref[b, :, Z_MQ + h * M_HD:Z_MQ + (h + 1) * M_HD] * (M_HD ** -0.5)
        k = z_ref[b, :, Z_MK + h * M_HD:Z_MK + (h + 1) * M_HD]
        v = z_ref[b, :, Z_MV + h * M_HD:Z_MV + (h + 1) * M_HD]
        qb, kb, vb = q.astype(bf16), k.astype(bf16), v.astype(bf16)
        Cm_ = C_sc[b, h]
        qk = lax.dot_general(qb, kb, _NT, preferred_element_type=f32)
        qC = lax.dot_general(qb, Cm_.astype(bf16), _NT, preferred_element_type=f32)
        yield
        b_c = P["cum"][:, SM_MF + h:SM_MF + h + 1]
        b_r = P["cumT"][SM_MF + h:SM_MF + h + 1, :]
        li_c = P["LI"][:, SM_MI + h:SM_MI + h + 1]
        li_r = P["LIT"][SM_MI + h:SM_MI + h + 1, :]
        m_prev = m_sc[b, h:h + 1, 0:1]
        a = b_c + m_prev
        d = jnp.where(causal, b_c - b_r + li_r, NEG_INF)
        m_t = jnp.maximum(a, jnp.max(d, axis=1, keepdims=True))
        s = qk * jnp.exp(d - m_t)
        e_in = jnp.exp(a - m_t)
        sv = jnp.dot(s.astype(bf16), vb, preferred_element_type=f32)
        b_last = b_c[c - 1:c, :]
        g_c = b_last - b_c + li_c
        m_new = jnp.maximum(b_last + m_prev, jnp.max(g_c, axis=0, keepdims=True))
        e_c = jnp.exp(b_last + m_prev - m_new)
        wg = jnp.exp(g_c - m_new)
        upd = jnp.dot(xpose_b((wg * v).astype(bf16)), kb, preferred_element_type=f32)
        yield
        num = sv + e_in * qC
        n_row = n_sc[b, h:h + 1, :]
        den = jnp.sum(s, axis=1, keepdims=True) + e_in * jnp.sum(q * n_row, axis=1, keepdims=True)
        hh = num / jnp.maximum(jnp.abs(den), jnp.exp(-m_t))
        C_sc[b, h] = e_c * Cm_ + upd
        n_sc[b, h:h + 1, :] = e_c * n_row + jnp.sum(wg * k, axis=0, keepdims=True)
        m_sc[b, h:h + 1, :] = jnp.broadcast_to(m_new, (1, LANE))
        mu = jnp.mean(hh, axis=1, keepdims=True)
        xc = hh - mu
        hn = xc * lax.rsqrt(jnp.mean(xc * xc, axis=1, keepdims=True) + EPS)
        mo = z_ref[b, :, Z_MO + h * M_HD:Z_MO + (h + 1) * M_HD]
        br_ref[b, :, sl] = hn * mnorm[:, sl] * jax.nn.sigmoid(mo)

    def gla_seq(b, P):
        ga = jnp.dot(P["zs"].astype(bf16), gup[...], preferred_element_type=f32) + gab[...]
        yield
        log_a = jnp.where(tvalid, jax.nn.log_sigmoid(ga) * (1.0 / G_TAU), 0.0)
        lam = jnp.dot(tri, log_a, precision=_HI, preferred_element_type=f32)
        yield
        levels = []
        m_half = c // 2
        while m_half >= 1:
            sh = int(math.log2(2 * m_half))
            if 2 * m_half >= F32_ROWS:
                blk = lam.reshape(c // (2 * m_half), 2 * m_half, G_KW)[:, m_half - 1:m_half, :]
                beta = jnp.broadcast_to(blk, (c // (2 * m_half), 2 * m_half, G_KW)).reshape(c, G_KW)
            else:
                sel = (col == ((row >> sh) << sh) + (m_half - 1)).astype(f32)
                beta = jnp.dot(sel, lam, precision=_HI, preferred_element_type=f32)
            levels.append((m_half, sh, beta))
            m_half //= 2
        dec_c = [jnp.exp(xpose_f(lam[c - F32_ROWS:c, i * LANE:(i + 1) * LANE])[:, F32_ROWS - 1:F32_ROWS])
                 for i in range(G_KW // LANE)]
        gq = z_ref[b, :, Z_GQ:Z_GQ + G_KW] * (G_DK ** -0.5)
        gk = z_ref[b, :, Z_GK:Z_GK + G_KW]
        yield
        att = [None] * G_HEADS
        for m_half, sh, beta in levels:
            upper = (tcol & (2 * m_half - 1)) >= m_half
            qm = (gq * jnp.exp(jnp.where(upper, lam - beta, NEG_INF))).astype(bf16)
            km = (gk * jnp.exp(jnp.where(upper, NEG_INF, beta - lam))).astype(bf16)
            ps = [lax.dot_general(qm[:, h * G_DK:(h + 1) * G_DK], km[:, h * G_DK:(h + 1) * G_DK], _NT,
                                  preferred_element_type=f32) for h in range(G_HEADS)]
            yield
            if 2 * m_half < c:
                same_blk = (row >> sh) == (col >> sh)
                ps = [jnp.where(same_blk, p, 0.0) for p in ps]
            att = [p if a_ is None else a_ + p for a_, p in zip(att, ps)]
        q_in = (gq * jnp.exp(lam)).astype(bf16)
        k_out = (gk * jnp.exp(lam[c - 1:c, :] - lam)).astype(bf16)
        k_outT = [xpose_b(k_out[:, i * LANE:(i + 1) * LANE]) for i in range(G_KW // LANE)]
        hpl = LANE // G_DK
        for h in range(G_HEADS):
            ks = slice(h * G_DK, (h + 1) * G_DK)
            vs = slice(h * G_DV, (h + 1) * G_DV)
            v = z_ref[b, :, Z_GV + h * G_DV:Z_GV + (h + 1) * G_DV]
            vb = v.astype(bf16)
            S_ = S_sc[b, h]
            o_att = jnp.dot(att[h].astype(bf16), vb, preferred_element_type=f32)
            o_st = jnp.dot(q_in[:, ks], S_.astype(bf16), preferred_element_type=f32)
            rs = slice((h % hpl) * G_DK, (h % hpl + 1) * G_DK)
            upd = jnp.dot(k_outT[h // hpl][rs, :], vb, preferred_element_type=f32)
            yield
            diag = jnp.sum(gq[:, ks] * gk[:, ks], axis=1, keepdims=True)
            o = o_att + diag * v + o_st
            S_sc[b, h] = dec_c[h // hpl][rs, :] * S_ + upd
            on = o * lax.rsqrt(jnp.mean(o * o, axis=1, keepdims=True) + EPS)
            gr = z_ref[b, :, Z_GR + h * G_DV:Z_GR + (h + 1) * G_DV]
            br_ref[b, :, BR_W + h * G_DV:BR_W + (h + 1) * G_DV] = on * gnorm[:, vs] * jax.nn.silu(gr)

    def ssd_group(b, g, P, act):
        Bm = act[:, S_W + g * S_STATE:S_W + (g + 1) * S_STATE].astype(bf16)
        c0 = S_W + S_GROUPS * S_STATE + g * S_STATE
        Cmat = act[:, c0:c0 + S_STATE].astype(bf16)
        cbm = lax.dot_general(Cmat, Bm, _NT, preferred_element_type=f32)
        yield
        ys = []
        ssq = None
        for hg in range(S_HG):
            hh_ = g * S_HG + hg
            ps = slice(hh_ * S_HD, (hh_ + 1) * S_HD)
            xs = act[:, ps]
            dt_c = P["DT"][:, SM_DT + hh_:SM_DT + hh_ + 1]
            dt_r = P["DTT"][hh_:hh_ + 1, :]
            l_c = P["cum"][:, SM_DT + hh_:SM_DT + hh_ + 1]
            l_r = P["cumT"][SM_DT + hh_:SM_DT + hh_ + 1, :]
            decay = jnp.exp(jnp.where(causal, l_c - l_r, NEG_INF))
            w = (cbm * decay * dt_r).astype(bf16)
            hs_ = h_sc[b, hh_]
            l_last = l_c[c - 1:c, :]
            ws = jnp.exp(l_last - l_c) * dt_c
            y_in = jnp.dot(w, xs.astype(bf16), preferred_element_type=f32)
            y_st = lax.dot_general(Cmat, hs_.astype(bf16), _NT, preferred_element_type=f32)
            upd = jnp.dot(xpose_b((ws * xs).astype(bf16)), Bm, preferred_element_type=f32)
            yield
            y = y_in + y_st * jnp.exp(l_c)
            h_sc[b, hh_] = jnp.exp(l_last) * hs_ + upd
            yy = (y + drow[:, ps] * xs) * jax.nn.silu(z_ref[b, :, Z_SZ + hh_ * S_HD:Z_SZ + (hh_ + 1) * S_HD])
            ys.append(yy)
            sq = jnp.sum(yy * yy, axis=1, keepdims=True)
            ssq = sq if ssq is None else ssq + sq
        scale = lax.rsqrt(ssq * (1.0 / (S_W // S_GROUPS)) + EPS)
        for hg in range(S_HG):
            hh_ = g * S_HG + hg
            ps = slice(hh_ * S_HD, (hh_ + 1) * S_HD)
            br_ref[b, :, 2 * BR_W + hh_ * S_HD:2 * BR_W + (hh_ + 1) * S_HD] = ys[hg] * scale * snorm[:, ps]

    streams = []
    for b in range(nb):
        P = prologue(b)
        xp_sc[b, 8:8 + c, :] = z_ref[b, :, Z_XBC:Z_XBC + S_XBC]
        conv = cb[...] + cw[0:1, :] * xp_sc[b, pl.ds(5, c), :]
        for jj in range(1, S_CONV):
            conv = conv + cw[jj:jj + 1, :] * xp_sc[b, pl.ds(5 + jj, c), :]
        new_buf = xp_sc[b, pl.ds(5 + valid, S_CONV - 1), :]
        xp_sc[b, 5:8, :] = new_buf
        act = jax.nn.silu(conv)
        streams += [mlstm_head(b, h, P) for h in range(M_HEADS)]
        streams += [gla_seq(b, P)]
        streams += [ssd_group(b, g, P, act) for g in range(S_GROUPS)]
    while streams:
        alive = []
        for st in streams:
            try:
                next(st)
                alive.append(st)
            except StopIteration:
                pass
        streams = alive

    @pl.when(j == pl.num_programs(1) - 1)
    def _store_state():
        C1[...] = C_sc[...]
        n1[...] = n_sc[...]
        m1[...] = m_sc[...]
        S1[...] = S_sc[...]
        h1[...] = h_sc[...]
        buf1[...] = xp_sc[:, 5:8, :]


def _mixer(z, states, layer, W, c, valid, nb):
    B, L, _ = z.shape
    nc = L // c
    C0, n0, m0, S0, h0, buf0 = states
    m0 = m0.reshape(m0.shape[0], B, 1, M_HEADS)

    def per_b(shape):
        nd = len(shape)
        return pl.BlockSpec((nb,) + shape, lambda b, j: (b,) + (0,) * nd)

    def per_b_in(shape):
        nd = len(shape)
        return pl.BlockSpec((None, nb) + shape, lambda b, j: (layer, b) + (0,) * nd)

    def const(shape):
        nd = len(shape)
        return pl.BlockSpec(shape, lambda b, j: (0,) * nd)

    small_w = (W["bias_s"], W["alog"], W["mnorm"], W["gnorm"], W["snorm"], W["drow"],
               W["gup"], W["gab"], W["cw"], W["cb"])
    st_shapes = [(M_HEADS, M_HD, M_HD), (M_HEADS, M_HD), (M_HEADS, LANE),
                 (G_HEADS, G_DK, G_DV), (S_HEADS, S_HD, S_STATE), (S_CONV - 1, S_XBC)]
    out = pl.pallas_call(
        functools.partial(_mixer_body, c, valid, nb),
        grid=(B // nb, nc),
        in_specs=[pl.BlockSpec((nb, c, Z_W), lambda b, j: (b, j, 0)),
                  per_b_in((M_HEADS, M_HD, M_HD)), per_b_in((M_HEADS, M_HD)), per_b_in((1, M_HEADS)),
                  per_b_in((G_HEADS, G_DK, G_DV)), per_b_in((S_HEADS, S_HD, S_STATE)),
                  per_b_in((S_CONV - 1, S_XBC))]
                 + [const(w.shape) for w in small_w],
        out_specs=[pl.BlockSpec((nb, c, N_BRANCH * BR_W), lambda b, j: (b, j, 0))]
                  + [per_b(s) for s in st_shapes],
        out_shape=[jax.ShapeDtypeStruct((B, L, N_BRANCH * BR_W), f32)]
                  + [jax.ShapeDtypeStruct((B,) + s, f32) for s in st_shapes],
        scratch_shapes=[pltpu.VMEM((nb, M_HEADS, M_HD, M_HD), f32), pltpu.VMEM((nb, M_HEADS, M_HD), f32),
                        pltpu.VMEM((nb, M_HEADS, LANE), f32), pltpu.VMEM((nb, G_HEADS, G_DK, G_DV), f32),
                        pltpu.VMEM((nb, S_HEADS, S_HD, S_STATE), f32), pltpu.VMEM((nb, c + 8, S_XBC), f32)],
        compiler_params=_cparams("parallel", "arbitrary"),
        name="mixer",
    )(z, C0, n0, m0, S0, h0, buf0, *small_w)
    br, C1, n1, m1, S1, h1, buf1 = out
    return br, (C1, n1, m1[:, :, 0], S1, h1, buf1)


def _layernorm(x, g, b):
    mu = jnp.mean(x, axis=1, keepdims=True)
    xc = x - mu
    return xc * lax.rsqrt(jnp.mean(xc * xc, axis=1, keepdims=True) + EPS) * g + b


def _merge_body(x_ref, br_ref, wg_ref, wb_ref, wo_ref, g_ref, b_ref, o_ref):
    x = x_ref[...]
    xb = x.astype(bf16)
    mixed = None
    for n in range(N_BRANCH):
        gate = jnp.dot(xb, wg_ref[:, n * D_MODEL:(n + 1) * D_MODEL], preferred_element_type=f32)
        proj = jnp.dot(br_ref[:, n * BR_W:(n + 1) * BR_W].astype(bf16), wb_ref[n], preferred_element_type=f32)
        t = jax.nn.sigmoid(gate) * proj
        mixed = t if mixed is None else mixed + t
    y = jnp.dot(mixed.astype(bf16), wo_ref[...], preferred_element_type=f32)
    o_ref[...] = _layernorm(DN_ALPHA * x + y, g_ref[...], b_ref[...])


def _merge(x, br, W, layer, tm):
    T = x.shape[0]
    full = lambda a: pl.BlockSpec((None,) + a.shape[1:], lambda i: (layer,) + (0,) * (a.ndim - 1))
    ws = (W["w_gate"], W["w_branch"], W["w_out"], W["ln1_g"], W["ln1_b"])
    return pl.pallas_call(
        _merge_body,
        grid=(T // tm,),
        in_specs=[pl.BlockSpec((tm, D_MODEL), lambda i: (i, 0)),
                  pl.BlockSpec((tm, N_BRANCH * BR_W), lambda i: (i, 0))] + [full(w) for w in ws],
        out_specs=pl.BlockSpec((tm, D_MODEL), lambda i: (i, 0)),
        out_shape=jax.ShapeDtypeStruct((T, D_MODEL), f32),
        compiler_params=_cparams("parallel"),
        name="merge",
    )(x, br, *ws)


ROUTE_TT = 128
_STAIR = [(k1, k2) for k1 in range(P_TOPK) for k2 in range(P_TOPK // (k1 + 1))]
N_CAND = -(-len(_STAIR) // 8) * 8


def _cand_tables():
    p12 = np.zeros((N_CAND, 2 * P_TOPK), np.float32)
    pos = np.full((N_CAND, LANE), 4.0 * P_TOPK * P_TOPK, np.float32)
    bias = np.full((N_CAND, LANE), NEG_INF, np.float32)
    for r, (k1, k2) in enumerate(_STAIR):
        p12[r, k1] = 1.0
        p12[r, P_TOPK + k2] = 1.0
        pos[r, :] = k1 * P_TOPK + k2
        bias[r, :] = 0.0
    return jnp.asarray(p12), jnp.asarray(pos), jnp.asarray(bias)


def _route_body(q_ref, keys_ref, p12_ref, cpos_ref, cbias_ref, r2_ref, e2_ref, lim_ref, cw_ref, sc_sc, v_sc):
    tt = q_ref.shape[1]
    rowf = lax.broadcasted_iota(i32, (P_NKEYS, tt), 0).astype(f32)
    k1row = lax.broadcasted_iota(i32, (P_TOPK, 1), 0).astype(f32)
    cpos = cpos_ref[...]

    for g in range(2 * P_HEADS):
        sc_sc[g] = lax.dot_general(keys_ref[g], q_ref[g].astype(bf16), _NT, preferred_element_type=f32)

    def run(tie_safe):
        bad = jnp.zeros((1, tt), f32)
        A = B = C = None
        for step in range(P_HEADS + 2):
            if step < P_HEADS:
                A = dict(h=step, work=sc_sc[pl.ds(2 * step, 2)],
                         rank0=jnp.full((P_NKEYS, tt), float(P_TOPK), f32),
                         rank1=jnp.full((P_NKEYS, tt), float(P_TOPK), f32))
            else:
                A = None
            for k in range(P_TOPK):
                if A is not None:
                    work = A["work"]
                    m = jnp.max(work, axis=1, keepdims=True)
                    if tie_safe:
                        idx = jnp.min(jnp.where(work == m, rowf[None], float(P_NKEYS)), axis=1, keepdims=True)
                        hit = rowf[None] == idx
                        A["rank0"] = jnp.where(hit[0], float(k), A["rank0"])
                    else:
                        hit = work == m
                    A["rank1"] = jnp.where(hit[1], float(k), A["rank1"])
                    A["work"] = jnp.where(hit, NEG_INF, work)
                    v_sc[A["h"] % 2, 0, k:k + 1, :] = m[0]
                    v_sc[A["h"] % 2, 1, k:k + 1, :] = m[1]
                if B is not None:
                    cand = B["cand"]
                    m = jnp.max(cand, axis=0, keepdims=True)
                    ps = jnp.min(jnp.where(cand == m, cpos, 4.0 * P_TOPK * P_TOPK), axis=0, keepdims=True)
                    B["cand"] = jnp.where(cpos == ps, NEG_INF, cand)
                    B["cnt"] = B["cnt"] + jnp.where(k1row == jnp.floor(ps * (1.0 / P_TOPK)), 1.0, 0.0)
                    if k == 0:
                        B["top0"] = m
                        B["zsum"] = jnp.ones_like(m)
                    else:
                        B["zsum"] = B["zsum"] + jnp.exp(m - B["top0"])
                if C is not None:
                    at_rank_k = (C["rank0"] == float(k)) if tie_safe else (C["sc1"] == C["v1"][k:k + 1, :])
                    C["lim"] = jnp.where(at_rank_k, C["cnt"][k:k + 1, :], C["lim"])
            if C is not None:
                h = C["h"]
                lim_ref[h, 0] = C["lim"]
                cw_ref[h, 0] = jnp.exp(sc_sc[2 * h] - C["top1"]) / C["zsum"]
            C = None
            if B is not None:
                C = dict(h=B["h"], rank0=B["rank0"], sc1=sc_sc[2 * B["h"]], v1=B["v1"], cnt=B["cnt"],
                         zsum=B["zsum"], top1=B["top1"], lim=jnp.zeros((P_NKEYS, tt), f32))
            B = None
            if A is not None:
                h = A["h"]
                if not tie_safe:
                    for gone in (A["work"][0] == NEG_INF, A["rank1"] < float(P_TOPK)):
                        removed = jnp.sum(jnp.where(gone, 1.0, 0.0), axis=0, keepdims=True)
                        bad = jnp.maximum(bad, jnp.where(removed == float(P_TOPK), 0.0, 1.0))
                v1 = v_sc[h % 2, 0]
                v2 = v_sc[h % 2, 1]
                r2_ref[h] = A["rank1"].astype(bf16)
                e2_ref[h] = jnp.exp(sc_sc[2 * h + 1] - v2[0:1, :]).astype(bf16)
                cand = (jnp.dot(p12_ref[:, 0:P_TOPK], v1, precision=_HI, preferred_element_type=f32)
                        + jnp.dot(p12_ref[:, P_TOPK:2 * P_TOPK], v2, precision=_HI, preferred_element_type=f32)
                        + cbias_ref[...])
                B = dict(h=h, cand=cand, cnt=jnp.zeros((P_TOPK, tt), f32), rank0=A["rank0"], v1=v1, top1=v1[0:1, :])
        return bad

    bad = run(False)

    @pl.when(jnp.max(bad) > 0.0)
    def _exact_ties():
        run(True)


def _route(q16, keys):
    G, T, _ = q16.shape
    tt = ROUTE_TT
    tabs = _cand_tables()
    spec = pl.BlockSpec((P_HEADS, P_NKEYS, tt), lambda i: (0, 0, i))
    full = lambda a: pl.BlockSpec(a.shape, lambda i: (0,) * a.ndim)
    return pl.pallas_call(
        _route_body,
        grid=(T // tt,),
        in_specs=[pl.BlockSpec((G, tt, P_HALF), lambda i: (0, i, 0)), full(keys)] + [full(t) for t in tabs],
        out_specs=[spec] * 2 + [pl.BlockSpec((P_HEADS, 1, P_NKEYS, tt), lambda i: (0, i, 0, 0))] * 2,
        out_shape=[jax.ShapeDtypeStruct((P_HEADS, P_NKEYS, T), bf16)] * 2
                  + [jax.ShapeDtypeStruct((P_HEADS, T // tt, P_NKEYS, tt), f32)] * 2,
        scratch_shapes=[pltpu.VMEM((2 * P_HEADS, P_NKEYS, tt), f32), pltpu.VMEM((2, 2, P_TOPK, tt), f32)],
        compiler_params=_cparams("parallel"),
        name="peer_route",
    )(q16, keys, *tabs)


PEER_NA = 4
PEER_EB = PEER_NA * P_NKEYS
BF16_ROWS = 16
F32_ROWS = 8


def _peer_weights(act, row0, r2_ref, e2_ref, lim_ref, cw_ref, w_sc):
    tt = act.shape[1]
    grp = (P_NKEYS // BF16_ROWS, BF16_ROWS, tt)
    zero = jnp.zeros((), bf16)
    for al in range(PEER_NA):
        gate = None
        for h in range(P_HEADS):
            r = row0 + al
            rep = lambda ref, j: jnp.broadcast_to(ref[h, j, r:r + 1, :], (F32_ROWS, LANE))
            lim = jnp.concatenate([rep(lim_ref, j) for j in range(tt // LANE)], axis=1)
            cw = jnp.concatenate([rep(cw_ref, j) for j in range(tt // LANE)], axis=1)
            lim = jnp.concatenate([lim, lim], axis=0).astype(bf16)
            cw = jnp.concatenate([cw, cw], axis=0).astype(bf16)
            t = jnp.where(r2_ref[h].reshape(grp) < lim[None], e2_ref[h].reshape(grp), zero) * cw[None]
            gate = t if gate is None else gate + t
        a = act[al * P_NKEYS:(al + 1) * P_NKEYS, :].astype(bf16)
        gelu = 0.5 * a * (1.0 + lax.erf(a * (2.0 ** -0.5)))
        w_sc[(row0 + al) * P_NKEYS:(row0 + al + 1) * P_NKEYS, :] = (gelu.reshape(grp) * gate).reshape(P_NKEYS, tt)


PEER_BPS = 4


def _peer_body(x_ref, u0_ref, u1_ref, u2_ref, u3_ref, un_ref, vt_ref,
               r2_ref, e2_ref, lim_ref, cw_ref, g_ref, b_ref, o_ref, xb_sc, acc_sc, actx_sc, acty_sc, w_sc):
    s = pl.program_id(1)
    route = (r2_ref, e2_ref, lim_ref, cw_ref)

    @pl.when(s == 0)
    def _init():
        xb0 = x_ref[...].astype(bf16)
        xb_sc[...] = xb0
        acc_sc[...] = jnp.zeros_like(acc_sc)
        actx_sc[...] = lax.dot_general(u0_ref[...], xb0, _NT, preferred_element_type=f32)

    xb = xb_sc[...]
    first = lambda u_ref: lax.dot_general(u_ref[...], xb, _NT, preferred_element_type=f32)
    act1 = first(u1_ref)
    acty_sc[...] = first(u2_ref)
    _peer_weights(actx_sc[...], 0, *route, w_sc)
    _peer_weights(act1, PEER_NA, *route, w_sc)
    act3 = first(u3_ref)
    actx_sc[...] = first(un_ref)
    _peer_weights(acty_sc[...], 2 * PEER_NA, *route, w_sc)
    _peer_weights(act3, 3 * PEER_NA, *route, w_sc)
    acc_sc[...] += jnp.dot(vt_ref[...], w_sc[...], preferred_element_type=f32)

    @pl.when(s == pl.num_programs(1) - 1)
    def _finish():
        y = acc_sc[...].T
        o_ref[...] = _layernorm(DN_ALPHA * x_ref[...] + y, g_ref[...], b_ref[...])


def _peer(x, route, W, layer, tt):
    T = x.shape[0]
    r2, e2, lim, cw = route
    nblk = P_EXPERTS // PEER_EB
    ns = nblk // PEER_BPS
    tok = pl.BlockSpec((P_HEADS, P_NKEYS, tt), lambda i, s: (0, 0, i))
    per_a = pl.BlockSpec((P_HEADS, tt // LANE, PEER_BPS * PEER_NA, LANE), lambda i, s: (0, i, s, 0))
    vec = pl.BlockSpec((None, 1, D_MODEL), lambda i, s: (layer, 0, 0))
    u_blk = lambda k: pl.BlockSpec((None, PEER_EB, D_MODEL),
                                   lambda i, s: (layer, jnp.minimum(PEER_BPS * s + k, nblk - 1), 0))
    vt_blk = pl.BlockSpec((None, D_MODEL, PEER_BPS * PEER_EB), lambda i, s: (layer, 0, s))
    return pl.pallas_call(
        _peer_body,
        grid=(T // tt, ns),
        in_specs=[pl.BlockSpec((tt, D_MODEL), lambda i, s: (i, 0))]
                 + [u_blk(k) for k in range(PEER_BPS + 1)] + [vt_blk]
                 + [tok, tok, per_a, per_a, vec, vec],
        out_specs=pl.BlockSpec((tt, D_MODEL), lambda i, s: (i, 0)),
        out_shape=jax.ShapeDtypeStruct((T, D_MODEL), f32),
        scratch_shapes=[pltpu.VMEM((tt, D_MODEL), bf16), pltpu.VMEM((D_MODEL, tt), f32),
                        pltpu.VMEM((PEER_EB, tt), f32), pltpu.VMEM((PEER_EB, tt), f32),
                        pltpu.VMEM((PEER_BPS * PEER_EB, tt), bf16)],
        compiler_params=_cparams("parallel", "arbitrary"),
        name="peer_experts",
    )(x, *([W["p_u"]] * (PEER_BPS + 1)), W["p_vt"], r2, e2, lim, cw, W["ln2_g"], W["ln2_b"])


_IN_SIZES = (512, 512, 512, 512, 4, 4, 256, 256, 512, 512, 16, 512, 768, 8, 3072)


def _prep_small(m_i_bias, m_f_bias, m_norm, g_a_up, g_a_bias, g_norm,
                s_conv_w, s_conv_b, s_dt_bias, s_A_log, s_D, s_norm):
    zl = lambda n: jnp.zeros((n,), f32)
    row = lambda a: a.astype(f32).reshape(1, -1)
    return dict(
        bias_s=row(jnp.concatenate([m_i_bias, m_f_bias, zl(G_RANK), s_dt_bias, zl(LANE - SM_DT - S_HEADS)])),
        alog=row(jnp.concatenate([zl(SM_DT), s_A_log, zl(LANE - SM_DT - S_HEADS)])),
        mnorm=row(m_norm), gnorm=row(g_norm), snorm=row(s_norm),
        drow=row(jnp.repeat(s_D, S_HD)),
        gup=jnp.zeros((LANE, G_KW), f32).at[SM_GA:SM_GA + G_RANK].set(g_a_up).astype(bf16),
        gab=row(g_a_bias),
        cw=s_conv_w.astype(f32), cb=row(s_conv_b),
    )


def _prep_big(w_in, w_branch, w_out, ln1_g, ln1_b, p_wq, p_keys, p_u, p_v, ln2_g, ln2_b):
    offs = [0]
    for s in _IN_SIZES:
        offs.append(offs[-1] + s)
    colw = lambda i: w_in[:, :, offs[i]:offs[i + 1]]
    (mq, mk, mv, mo, mi, mf, gq, gk, gv, gr, ga, sz, sxbc, sdt, gate) = [colw(i) for i in range(15)]
    pad = jnp.zeros((DEPTH, D_MODEL, LANE - (SM_DT + S_HEADS)), w_in.dtype)
    w1 = jnp.concatenate([mq, mk, mv, mo, gq, gk, gv, gr, sz, sxbc, mi, mf, ga, sdt, pad], axis=2)
    rows = lambda a: a.astype(f32).reshape(DEPTH, 1, -1)
    return dict(
        w1=w1.astype(bf16), w_gate=gate.astype(bf16),
        w_branch=w_branch.astype(bf16), w_out=w_out.astype(bf16),
        ln1_g=rows(ln1_g), ln1_b=rows(ln1_b),
        p_wq=p_wq.astype(bf16),
        p_keys=p_keys.astype(bf16).reshape(DEPTH, 2 * P_HEADS, P_NKEYS, P_HALF),
        p_u=p_u.astype(bf16), p_vt=jnp.swapaxes(p_v.astype(bf16), 1, 2),
        ln2_g=rows(ln2_g), ln2_b=rows(ln2_b),
    )


def _run_layer(x, states, slab, W, Ws, layer, c, valid, nb, tm, peer_tt):
    B, L, _ = x.shape
    nchunks = L // valid
    Lp = nchunks * c
    if c != valid:
        xp = jnp.pad(x.reshape(B, nchunks, valid, D_MODEL), ((0, 0), (0, 0), (0, c - valid), (0, 0)))
        xp = xp.reshape(B * Lp, D_MODEL)
    else:
        xp = x.reshape(B * L, D_MODEL)
    z = _matmul(xp, W["w1"], layer, tm).reshape(B, Lp, Z_W)
    br, new_states = _mixer(z, states, slab, Ws, c, valid, nb)
    if c != valid:
        br = br.reshape(B, nchunks, c, N_BRANCH * BR_W)[:, :, :valid]
    xt = x.reshape(B * L, D_MODEL)
    h = _merge(xt, br.reshape(B * L, N_BRANCH * BR_W), W, layer, tm)
    q16 = _wq_matmul(h, W["p_wq"], layer, tm)
    route = _route(q16, W["p_keys"][layer])
    y = _peer(h, route, W, layer, peer_tt)
    return y.reshape(B, L, D_MODEL), new_states


def _zero_states(b):
    return (jnp.zeros((1, b, M_HEADS, M_HD, M_HD), f32), jnp.zeros((1, b, M_HEADS, M_HD), f32),
            jnp.zeros((1, b, M_HEADS), f32), jnp.zeros((1, b, G_HEADS, G_DK, G_DV), f32),
            jnp.zeros((1, b, S_HEADS, S_HD, S_STATE), f32), jnp.zeros((1, b, S_CONV - 1, S_XBC), f32))


PROMPT_CHUNK = 128
SAMPLE_CHUNK = 16
PROMPT_NB = 2
SAMPLE_NB = 8
PROJ_TM = 512
PEER_TT = 512


def kernel(x_prompt, x_sample, state_mlstm_C, state_mlstm_n, state_mlstm_m, state_gla_S, state_ssm_h, state_conv, w_in, m_i_bias, m_f_bias, m_norm, g_a_up, g_a_bias, g_norm, s_conv_w, s_conv_b, s_dt_bias, s_A_log, s_D, s_norm, w_branch, w_out, ln1_g, ln1_b, p_wq, p_keys, p_u, p_v, ln2_g, ln2_b):
    small = (m_i_bias, m_f_bias, m_norm, g_a_up, g_a_bias, g_norm,
             s_conv_w, s_conv_b, s_dt_bias, s_A_log, s_D, s_norm)
    W = _prep_big(w_in, w_branch, w_out, ln1_g, ln1_b, p_wq, p_keys, p_u, p_v, ln2_g, ln2_b)
    in_states = (state_mlstm_C, state_mlstm_n, state_mlstm_m, state_gla_S, state_ssm_h, state_conv)
    hp, hs = x_prompt, x_sample
    ls = x_sample.shape[1]
    new_p = [[] for _ in range(6)]
    new_s = [[] for _ in range(6)]
    for l in range(DEPTH):
        Ws = _prep_small(*[w[l] for w in small])
        hp, sp = _run_layer(hp, _zero_states(hp.shape[0]), 0, W, Ws, l, PROMPT_CHUNK, PROMPT_CHUNK, PROMPT_NB,
                            PROJ_TM, PEER_TT)
        hs, ss = _run_layer(hs, in_states, l, W, Ws, l, SAMPLE_CHUNK, ls, SAMPLE_NB, PROJ_TM, PEER_TT)
        for j in range(6):
            new_p[j].append(sp[j])
            new_s[j].append(ss[j])
    P = [jnp.stack(a, axis=0) for a in new_p]
    S = [jnp.stack(a, axis=0) for a in new_s]
    return (hp, hs, P[0], P[1], P[2], P[3], P[4], P[5], S[0], S[1], S[2], S[3], S[4], S[5])
```
